```python
import math
import jax, jax.numpy as jnp
from jax import lax
import numpy as np

D_MODEL = 2048
BATCH = 16
SEQ = 2048
DEPTH = 2

N_MEM = 256
GRID_W = 64
HEAD_DIM = 64
EPS = 1e-6
NEG = -1e30
A_HEADS = 12
A_KV_HEADS = 4
A_WINDOW = 128
A_BLOCK = 128
B_HEADS = 8
NA_MAX_KH = 8
NA_KW = 16
NA_QCB = 16
NA_KCB = 32
C_HEADS = 12
C_KV_HEADS = 4
C_BLOCK = 128
ROPE_THETA = 10000.0
REL_BUCKETS = 32
REL_MAX_DIST = 128
MEM_HEADS = 4
MEM_HEAD_DIM = 128
MEM_W = MEM_HEADS * MEM_HEAD_DIM
PEER_HEADS = 8
PEER_NKEYS = 128
PEER_N = PEER_NKEYS * PEER_NKEYS
PEER_DKEY = 256
PEER_TOPK = 16
PEER_CHUNK = 128
A_Q = A_HEADS * HEAD_DIM
A_KV = A_KV_HEADS * HEAD_DIM
B_W = B_HEADS * HEAD_DIM
C_Q = C_HEADS * HEAD_DIM
C_KV = C_KV_HEADS * HEAD_DIM
D_MIX = A_Q + B_W + C_Q
D_IN = A_Q + 2 * A_KV + 3 * B_W + C_Q + 2 * C_KV

kernel_name = "hybrid_parallel_heads_peer_encoder"


def rmsnorm(x, g):
    xf = x.astype(jnp.float32)
    y = xf * lax.rsqrt(jnp.mean(xf * xf, axis=-1, keepdims=True) + EPS)
    return (y * g.astype(jnp.float32)).astype(x.dtype)


def t5_buckets(rel):
    nb = REL_BUCKETS // 2
    max_exact = nb // 2
    n = np.abs(rel)
    large = max_exact + (np.log(np.maximum(n, 1) / max_exact) / np.log(REL_MAX_DIST / max_exact)
                         * (nb - max_exact)).astype(np.int64)
    large = np.minimum(large, nb - 1)
    return np.where(rel > 0, nb, 0) + np.where(n < max_exact, n, large)


def window_attention(q, k, v, sink, rel_bias):
    b, s, _, dh = q.shape
    g = A_HEADS // A_KV_HEADS
    nb = s // A_BLOCK
    span = A_BLOCK + 2 * A_WINDOW
    kp = jnp.pad(k, ((0, 0), (A_WINDOW, A_WINDOW), (0, 0), (0, 0)))
    vp = jnp.pad(v, ((0, 0), (A_WINDOW, A_WINDOW), (0, 0), (0, 0)))
    rel = (np.arange(span)[None, :] - A_WINDOW) - np.arange(A_BLOCK)[:, None]
    in_window = np.abs(rel) <= A_WINDOW
    bias = rel_bias.astype(jnp.float32)[t5_buckets(rel)]
    bias = jnp.transpose(bias, (2, 0, 1)).reshape(A_KV_HEADS, g, A_BLOCK, span)
    sink_f = sink.astype(jnp.float32).reshape(A_KV_HEADS, g)
    qb = q.reshape(b, nb, A_BLOCK, A_KV_HEADS, g, dh).transpose(1, 0, 2, 3, 4, 5)
    scale = dh ** -0.5

    def block(args):
        qblk, bi = args
        start = bi * A_BLOCK
        kb = lax.dynamic_slice_in_dim(kp, start, span, axis=1)
        vb = lax.dynamic_slice_in_dim(vp, start, span, axis=1)
        logits = jnp.einsum('bqhgd,bkhd->bhgqk', qblk, kb).astype(jnp.float32) * scale + bias
        kpos = start - A_WINDOW + jnp.arange(span)
        valid = jnp.asarray(in_window) & (kpos >= 0)[None, :] & (kpos < s)[None, :]
        logits = jnp.where(valid, logits, NEG)
        sk = jnp.broadcast_to(sink_f[None, :, :, None, None], logits.shape[:-1] + (1,))
        p = jax.nn.softmax(jnp.concatenate([logits, sk], axis=-1), axis=-1)[..., :span]
        return jnp.einsum('bhgqk,bkhd->bqhgd', p.astype(v.dtype), vb)

    out = lax.map(block, (qb, jnp.arange(nb)))
    return out.transpose(1, 0, 2, 3, 4, 5).reshape(b, s, A_HEADS * dh)


def neighborhood_attention(q, k, v, rpb):
    b, s, h, dh = q.shape
    rows = s // GRID_W
    kh = min(NA_MAX_KH, rows)
    n_cb = GRID_W // NA_QCB
    slab = np.clip(np.arange(n_cb) * NA_QCB - NA_KW // 2, 0, GRID_W - NA_KCB)
    key_col = slab[:, None] + np.arange(NA_KCB)[None, :]
    q_col = np.arange(n_cb)[:, None] * NA_QCB + np.arange(NA_QCB)[None, :]
    c_start = np.clip(q_col - NA_KW // 2, 0, GRID_W - NA_KW)
    kc = key_col[:, None, :]
    col_ok = (kc >= c_start[:, :, None]) & (kc < c_start[:, :, None] + NA_KW)
    dc_idx = np.clip(kc - q_col[:, :, None] + NA_KW - 1, 0, 2 * NA_KW - 2)
    bias_c = rpb.astype(jnp.float32)[:, :, dc_idx]
    col_ok_b = jnp.asarray(col_ok[:, :, None, :])
    qg = q.reshape(b, rows, GRID_W, h, dh)
    kg = k.reshape(b, rows, GRID_W, h, dh)
    vg = v.reshape(b, rows, GRID_W, h, dh)
    scale = dh ** -0.5

    def one_row(r):
        r0 = jnp.clip(r - kh // 2, 0, rows - kh)
        qr = lax.dynamic_index_in_dim(qg, r, axis=1, keepdims=False).reshape(b, n_cb, NA_QCB, h, dh)
        kr = jnp.take(lax.dynamic_slice_in_dim(kg, r0, kh, axis=1), key_col, axis=2)
        vr = jnp.take(lax.dynamic_slice_in_dim(vg, r0, kh, axis=1), key_col, axis=2)
        logits = jnp.einsum('bcqhd,bickhd->bhcqik', qr, kr).astype(jnp.float32) * scale
        dr = r0 + jnp.arange(kh) - r + NA_MAX_KH - 1
        bias = jnp.take(bias_c, dr, axis=1).transpose(0, 2, 3, 1, 4)
        logits = jnp.where(col_ok_b, logits + bias[None], NEG)
        p = jax.nn.softmax(logits, axis=(-2, -1))
        o = jnp.einsum('bhcqik,bickhd->bcqhd', p.astype(v.dtype), vr)
        return o.reshape(b, GRID_W, h, dh)

    out = lax.map(one_row, jnp.arange(rows))
    return out.transpose(1, 0, 2, 3, 4).reshape(b, s, h * dh)


def rope_half(x, pos):
    n = x.shape[-1]
    nf = n // 2
    inv = jnp.asarray((ROPE_THETA ** (-np.arange(nf) * 2.0 / n)).astype(np.float32))
    ang = pos.astype(jnp.float32)[:, None] * inv[None, :]
    cos = jnp.cos(ang)[None, :, None, :]
    sin = jnp.sin(ang)[None, :, None, :]
    xf = x.astype(jnp.float32)
    x1, x2 = xf[..., :nf], xf[..., nf:]
    return jnp.concatenate([x1 * cos - x2 * sin, x1 * sin + x2 * cos], axis=-1).astype(x.dtype)


def axial_rope(x, row, col):
    half = x.shape[-1] // 2
    return jnp.concatenate([rope_half(x[..., :half], row), rope_half(x[..., half:], col)], axis=-1)


def dense_gqa(q, k, v):
    b, s, h, dh = q.shape
    hk = k.shape[2]
    g = h // hk
    nb = s // C_BLOCK
    qb = q.reshape(b, nb, C_BLOCK, hk, g, dh).transpose(1, 0, 2, 3, 4, 5)
    scale = dh ** -0.5

    def block(qblk):
        logits = jnp.einsum('bqhgd,bkhd->bhgqk', qblk, k).astype(jnp.float32) * scale
        p = jax.nn.softmax(logits, axis=-1)
        return jnp.einsum('bhgqk,bkhd->bqhgd', p.astype(v.dtype), v)

    out = lax.map(block, qb)
    return out.transpose(1, 0, 2, 3, 4, 5).reshape(b, s, h * dh)


def memory_attention(h, mem, w_q, w_kv, qk_g, w_o):
    b, s, _ = h.shape
    q = rmsnorm((h @ w_q).reshape(b, s, MEM_HEADS, MEM_HEAD_DIM), qk_g[0])
    kv = (mem @ w_kv).reshape(b, mem.shape[1], 2, MEM_HEADS, MEM_HEAD_DIM)
    k = rmsnorm(kv[:, :, 0], qk_g[1])
    vv = kv[:, :, 1]
    logits = jnp.einsum('bshd,bmhd->bhsm', q, k).astype(jnp.float32) * MEM_HEAD_DIM ** -0.5
    p = jax.nn.softmax(logits, axis=-1)
    o = jnp.einsum('bhsm,bmhd->bshd', p.astype(vv.dtype), vv)
    return o.reshape(b, s, MEM_W) @ w_o


def peer_ffn(h, w_q, sub_keys, u, v):
    b, s, d = h.shape
    nc = (b * s) // PEER_CHUNK

    def chunk(xc):
        q = (xc @ w_q).reshape(PEER_CHUNK, PEER_HEADS, 2, PEER_DKEY // 2)
        sc = jnp.einsum('chpd,hpnd->chpn', q, sub_keys).astype(jnp.float32)
        top_s, top_i = lax.top_k(sc, PEER_TOPK)
        cand_s = (top_s[:, :, 0, :, None] + top_s[:, :, 1, None, :]).reshape(PEER_CHUNK, PEER_HEADS, PEER_TOPK * PEER_TOPK)
        cand_i = (top_i[:, :, 0, :, None] * PEER_NKEYS + top_i[:, :, 1, None, :]).reshape(PEER_CHUNK, PEER_HEADS, PEER_TOPK * PEER_TOPK)
        best_s, pos = lax.top_k(cand_s, PEER_TOPK)
        expert = jnp.take_along_axis(cand_i, pos, axis=-1)
        gate = jax.nn.softmax(best_s, axis=-1)
        pre = jnp.einsum('cd,chkd->chk', xc, jnp.take(u, expert, axis=0)).astype(jnp.float32)
        w = (gate * jax.nn.gelu(pre, approximate=False)).astype(v.dtype)
        return jnp.einsum('chk,chkd->cd', w, jnp.take(v, expert, axis=0))

    y = lax.map(chunk, h.reshape(nc, PEER_CHUNK, d))
    return y.reshape(b, s, d)


def setup_inputs(seed: int = 0) -> dict:
    key = jax.random.key(seed)
    ks = jax.random.split(key, 24)
    nrm = lambda k, shape, scale: jax.random.normal(k, shape, jnp.float32) * scale
    gain = lambda k, shape: 1.0 + 0.02 * jax.random.normal(k, shape, jnp.float32)
    L = DEPTH
    return {
        "x": nrm(ks[0], (BATCH, SEQ, D_MODEL), 1.0),
        "mem": nrm(ks[1], (BATCH, N_MEM, D_MODEL), 1.0),
        "t5_rel_bias": nrm(ks[2], (REL_BUCKETS, A_HEADS), 0.1),
        "norm_mix": gain(ks[3], (L, D_MODEL)),
        "w_in": nrm(ks[4], (L, D_MODEL, D_IN), D_MODEL ** -0.5),
        "qk_norm_a": gain(ks[5], (L, 2, HEAD_DIM)),
        "sink_a": nrm(ks[6], (L, A_HEADS), 0.5),
        "qk_norm_b": gain(ks[7], (L, 2, HEAD_DIM)),
        "rpb_b": nrm(ks[8], (L, B_HEADS, 2 * NA_MAX_KH - 1, 2 * NA_KW - 1), 0.1),
        "qk_norm_c": gain(ks[9], (L, 2, HEAD_DIM)),
        "out_norm": gain(ks[10], (L, D_MIX)),
        "w_out": nrm(ks[11], (L, D_MIX, D_MODEL), D_MIX ** -0.5),
        "norm_mem": gain(ks[12], (L, D_MODEL)),
        "norm_mem_kv": gain(ks[13], (L, D_MODEL)),
        "w_mem_q": nrm(ks[14], (L, D_MODEL, MEM_W), D_MODEL ** -0.5),
        "w_mem_kv": nrm(ks[15], (L, D_MODEL, 2 * MEM_W), D_MODEL ** -0.5),
        "qk_norm_mem": gain(ks[16], (L, 2, MEM_HEAD_DIM)),
        "w_mem_o": nrm(ks[17], (L, MEM_W, D_MODEL), MEM_W ** -0.5),
        "norm_ffn": gain(ks[18], (L, D_MODEL)),
        "peer_w_q": nrm(ks[19], (L, D_MODEL, PEER_HEADS * PEER_DKEY), D_MODEL ** -0.5),
        "peer_keys": nrm(ks[20], (L, PEER_HEADS, 2, PEER_NKEYS, PEER_DKEY // 2), (PEER_DKEY // 2) ** -0.5),
        "peer_u": nrm(ks[21], (L, PEER_N, D_MODEL), D_MODEL ** -0.5),
        "peer_v": nrm(ks[22], (L, PEER_N, D_MODEL), PEER_HEADS ** -0.5),
    }


def reference(x, mem, t5_rel_bias, norm_mix, w_in, qk_norm_a, sink_a, qk_norm_b, rpb_b, qk_norm_c,
              out_norm, w_out, norm_mem, norm_mem_kv, w_mem_q, w_mem_kv, qk_norm_mem, w_mem_o,
              norm_ffn, peer_w_q, peer_keys, peer_u, peer_v):
    b, s, _ = x.shape
    t = jnp.arange(s)
    row, col = t // GRID_W, t % GRID_W
    splits = list(np.cumsum([A_Q, A_KV, A_KV, B_W, B_W, B_W, C_Q, C_KV]))
    heads = lambda z, n: z.reshape(b, s, n, HEAD_DIM)
    for l in range(DEPTH):
        hn = rmsnorm(x, norm_mix[l])
        qa, ka, va, qb, kb, vb, qc, kc, vc = jnp.split(hn @ w_in[l], splits, axis=-1)
        oa = window_attention(rmsnorm(heads(qa, A_HEADS), qk_norm_a[l, 0]),
                              rmsnorm(heads(ka, A_KV_HEADS), qk_norm_a[l, 1]),
                              heads(va, A_KV_HEADS), sink_a[l], t5_rel_bias)
        ob = neighborhood_attention(rmsnorm(heads(qb, B_HEADS), qk_norm_b[l, 0]),
                                    rmsnorm(heads(kb, B_HEADS), qk_norm_b[l, 1]),
                                    heads(vb, B_HEADS), rpb_b[l])
        oc = dense_gqa(axial_rope(rmsnorm(heads(qc, C_HEADS), qk_norm_c[l, 0]), row, col),
                       axial_rope(rmsnorm(heads(kc, C_KV_HEADS), qk_norm_c[l, 1]), row, col),
                       heads(vc, C_KV_HEADS))
        g_out = out_norm[l]
        mixed = jnp.concatenate([rmsnorm(oa, g_out[:A_Q]),
                                 rmsnorm(ob, g_out[A_Q:A_Q + B_W]),
                                 rmsnorm(oc, g_out[A_Q + B_W:])], axis=-1)
        x = x + mixed @ w_out[l]
        x = x + memory_attention(rmsnorm(x, norm_mem[l]), rmsnorm(mem, norm_mem_kv[l]),
                                 w_mem_q[l], w_mem_kv[l], qk_norm_mem[l], w_mem_o[l])
        x = x + peer_ffn(rmsnorm(x, norm_ffn[l]), peer_w_q[l], peer_keys[l], peer_u[l], peer_v[l])
    return x
```

```python
import functools

import numpy as np
import jax
import jax.numpy as jnp
from jax import lax
from jax.experimental import pallas as pl
from jax.experimental.pallas import tpu as pltpu

F32 = jnp.float32
BF16 = jnp.bfloat16

GRID_W = 64
HEAD_DIM = 64
EPS = 1e-6
NEG = -1e30
A_HEADS, A_KV_HEADS, A_WINDOW, A_BLOCK = 12, 4, 128, 128
B_HEADS, NA_MAX_KH, NA_KW = 8, 8, 16
C_HEADS, C_KV_HEADS, C_BLOCK = 12, 4, 128
ROPE_THETA = 10000.0
REL_BUCKETS, REL_MAX_DIST = 32, 128
MEM_HEADS, MEM_HEAD_DIM = 4, 128
MEM_W = MEM_HEADS * MEM_HEAD_DIM
PEER_HEADS, PEER_NKEYS, PEER_DKEY, PEER_TOPK = 8, 128, 256, 16
A_Q, A_KV = A_HEADS * HEAD_DIM, A_KV_HEADS * HEAD_DIM
B_W = B_HEADS * HEAD_DIM
C_Q, C_KV = C_HEADS * HEAD_DIM, C_KV_HEADS * HEAD_DIM
D_MIX = A_Q + B_W + C_Q
D_IN = A_Q + 2 * A_KV + 3 * B_W + C_Q + 2 * C_KV

LANE = 128
CHUNK = 256
VMEM_LIMIT = 56 * 1024 * 1024

_REF_SECTIONS = dict(qa=(0, A_Q), ka=(A_Q, A_KV), va=(A_Q + A_KV, A_KV),
                     qb=(A_Q + 2 * A_KV, B_W), kb=(A_Q + 2 * A_KV + B_W, B_W),
                     vb=(A_Q + 2 * A_KV + 2 * B_W, B_W),
                     qc=(A_Q + 2 * A_KV + 3 * B_W, C_Q),
                     kc=(A_Q + 2 * A_KV + 3 * B_W + C_Q, C_KV),
                     vc=(A_Q + 2 * A_KV + 3 * B_W + C_Q + C_KV, C_KV))
_PERM_ORDER = ("qa", "qc", "qb", "kb", "vb", "ka", "va", "kc", "vc")
_KIND = dict(qa="norm", qc="rope", qb="norm", kb="norm", vb="plain", ka="norm", va="plain",
             kc="rope", vc="plain")
_PERM_OFF = {}
_off = 0
for _n in _PERM_ORDER:
    _PERM_OFF[_n] = _off
    _off += _REF_SECTIONS[_n][1]
_CHUNK_KINDS = []
for _n in _PERM_ORDER:
    _CHUNK_KINDS += [_KIND[_n]] * (_REF_SECTIONS[_n][1] // CHUNK)


def _cparams(sem):
    return pltpu.CompilerParams(dimension_semantics=sem, vmem_limit_bytes=VMEM_LIMIT)


def _resident(shape):
    nd = len(shape)
    return pl.BlockSpec(shape, lambda *_: (0,) * nd, pipeline_mode=pl.Buffered(1))


def _rms(x, g):
    return x * lax.rsqrt(jnp.mean(x * x, axis=-1, keepdims=True) + EPS) * g


def _dot_nt(a, b):
    return lax.dot_general(a, b, (((1,), (1,)), ((), ())), preferred_element_type=F32)


def _inproj_kernel(*refs, has_y):
    if has_y:
        x_ref, y_ref, g_ref, w_ref, gain_ref, cos_ref, sin_ref, bd_ref, qkv_ref, xs_ref = refs
        x = x_ref[...] + y_ref[...]
        xs_ref[...] = x
    else:
        x_ref, g_ref, w_ref, gain_ref, cos_ref, sin_ref, bd_ref, qkv_ref = refs
        x = x_ref[...]
    hn = _rms(x, g_ref[...]).astype(BF16)
    lane = lax.broadcasted_iota(jnp.int32, (1, CHUNK), 1)
    lower_half = (lane % 32) < 16
    for c, kind in enumerate(_CHUNK_KINDS):
        cols = slice(c * CHUNK, (c + 1) * CHUNK)
        acc = jnp.dot(hn, w_ref[:, cols], preferred_element_type=F32)
        if kind != "plain":
            sq = acc * acc
            hi = sq.astype(BF16)
            lo = (sq - hi.astype(F32)).astype(BF16)
            ss = (jnp.dot(hi, bd_ref[...], preferred_element_type=F32)
                  + jnp.dot(lo, bd_ref[...], preferred_element_type=F32))
            acc = acc * lax.rsqrt(ss * (1.0 / HEAD_DIM) + EPS) * gain_ref[:, cols]
        if kind == "rope":
            partner = jnp.where(lower_half, pltpu.roll(acc, CHUNK - 16, 1), pltpu.roll(acc, 16, 1))
            acc = acc * cos_ref[...] + partner * sin_ref[...]
        qkv_ref[:, cols] = acc.astype(BF16)


def _inproj(x, y, g, w, gain, cos, sin, bd, seq, tm=256):
    t, d = x.shape
    n = w.shape[1]
    has_y = y is not None
    row = pl.BlockSpec((tm, d), lambda i: (i, 0))
    nseq = seq // tm
    tab = pl.BlockSpec((tm, CHUNK), lambda i: (i % nseq, 0))
    in_specs = [row] + ([row] if has_y else []) + [
        _resident((1, d)), _resident((d, n)), _resident((1, n)), tab, tab, _resident((CHUNK, CHUNK))]
    out_specs = [pl.BlockSpec((tm, n), lambda i: (i, 0))]
    out_shape = [jax.ShapeDtypeStruct((t, n), BF16)]
    if has_y:
        out_specs.append(row)
        out_shape.append(jax.ShapeDtypeStruct((t, d), F32))
    args = [x] + ([y] if has_y else []) + [g, w, gain, cos, sin, bd]
    outs = pl.pallas_call(
        functools.partial(_inproj_kernel, has_y=has_y), grid=(t // tm,), in_specs=in_specs,
        out_specs=out_specs, out_shape=out_shape, compiler_params=_cparams(("parallel",)),
        name="inproj")(*args)
    return (outs[0], outs[1]) if has_y else (outs[0], x)


def _attn_a_kernel(sink_ref, q_ref, kp_ref, kc_ref, kn_ref, vp_ref, vc_ref, vn_ref, bias_ref,
                   gout_ref, o_ref):
    j = pl.program_id(1)
    nb = pl.num_programs(1)
    q = q_ref[0]
    k = jnp.concatenate([kp_ref[0], kc_ref[0], kn_ref[0]], axis=0)
    v = jnp.concatenate([vp_ref[0], vc_ref[0], vn_ref[0]], axis=0)
    col = lax.broadcasted_iota(jnp.int32, (1, 3 * A_BLOCK), 1)
    first_ok = jnp.where(j > 0, 0, A_BLOCK)
    end_ok = jnp.where(j < nb - 1, 3 * A_BLOCK, 2 * A_BLOCK)
    ok = (col >= first_ok) & (col < end_ok)
    g = A_HEADS // A_KV_HEADS
    outs = []
    for hk in range(A_KV_HEADS):
        kh = k[:, hk * HEAD_DIM:(hk + 1) * HEAD_DIM]
        vh = v[:, hk * HEAD_DIM:(hk + 1) * HEAD_DIM]
        for gi in range(g):
            h = hk * g + gi
            qh = q[:, h * HEAD_DIM:(h + 1) * HEAD_DIM]
            logits = jnp.where(ok, _dot_nt(qh, kh) + bias_ref[h], NEG)
            s = sink_ref[h]
            m = jnp.maximum(jnp.max(logits, axis=-1, keepdims=True), s)
            p = jnp.exp(logits - m)
            denom = jnp.sum(p, axis=-1, keepdims=True) + jnp.exp(s - m)
            o = jnp.dot(p.astype(BF16), vh, preferred_element_type=F32)
            outs.append(o / denom)
    o = jnp.concatenate(outs, axis=-1)
    o_ref[0] = _rms(o, gout_ref[...]).astype(o_ref.dtype)


def _attn_a(qkv, sink, bias, gout):
    b, s, _ = qkv.shape
    nb = s // A_BLOCK
    kblk, vblk = _PERM_OFF["ka"] // A_KV, _PERM_OFF["va"] // A_KV
    qspec = pl.BlockSpec((1, A_BLOCK, A_Q), lambda bi, j: (bi, j, _PERM_OFF["qa"] // A_Q))

    def kv(blk, shift):
        return pl.BlockSpec((1, A_BLOCK, A_KV),
                            lambda bi, j: (bi, jnp.clip(j + shift, 0, nb - 1), blk))
    return pl.pallas_call(
        _attn_a_kernel, grid=(b, nb),
        in_specs=[pl.BlockSpec(memory_space=pltpu.SMEM), qspec,
                  kv(kblk, -1), kv(kblk, 0), kv(kblk, 1), kv(vblk, -1), kv(vblk, 0), kv(vblk, 1),
                  _resident(bias.shape), _resident((1, A_Q))],
        out_specs=pl.BlockSpec((1, A_BLOCK, A_Q), lambda bi, j: (bi, j, 0)),
        out_shape=jax.ShapeDtypeStruct((b, s, A_Q), BF16),
        compiler_params=_cparams(("parallel", "parallel")), name="attn_a",
    )(sink, qkv, qkv, qkv, qkv, qkv, qkv, qkv, bias, gout)


def _attn_b_kernel(q_ref, k_ref, v_ref, bias_ref, gout_ref, o_ref, *, rows, kh):
    r = pl.program_id(1)
    r0 = jnp.clip(r - kh // 2, 0, rows - kh)
    start = pl.multiple_of(r0 * GRID_W, GRID_W)
    q = q_ref[0]
    k = k_ref[0, pl.ds(start, kh * GRID_W), :]
    v = v_ref[0, pl.ds(start, kh * GRID_W), :]
    outs = []
    for h in range(B_HEADS):
        hs = slice(h * HEAD_DIM, (h + 1) * HEAD_DIM)
        logits = _dot_nt(q[:, hs], k[:, hs]) + bias_ref[h, 0]
        m = jnp.max(logits, axis=-1, keepdims=True)
        p = jnp.exp(logits - m)
        denom = jnp.sum(p, axis=-1, keepdims=True)
        outs.append(jnp.dot(p.astype(BF16), v[:, hs], preferred_element_type=F32) / denom)
    o = jnp.concatenate(outs, axis=-1)
    o_ref[0] = _rms(o, gout_ref[...]).astype(o_ref.dtype)


def _attn_b(qkv, bias, gout):
    b, s, _ = qkv.shape
    rows = s // GRID_W
    kh = min(NA_MAX_KH, rows)

    def d0(r):
        return jnp.clip(r - kh // 2, 0, rows - kh) - r + NA_MAX_KH - 1
    return pl.pallas_call(
        functools.partial(_attn_b_kernel, rows=rows, kh=kh), grid=(b, rows),
        in_specs=[pl.BlockSpec((1, GRID_W, B_W), lambda bi, r: (bi, r, _PERM_OFF["qb"] // B_W)),
                  pl.BlockSpec((1, s, B_W), lambda bi, r: (bi, 0, _PERM_OFF["kb"] // B_W)),
                  pl.BlockSpec((1, s, B_W), lambda bi, r: (bi, 0, _PERM_OFF["vb"] // B_W)),
                  pl.BlockSpec((B_HEADS, 1, GRID_W, kh * GRID_W), lambda bi, r: (0, d0(r), 0, 0)),
                  _resident((1, B_W))],
        out_specs=pl.BlockSpec((1, GRID_W, B_W), lambda bi, r: (bi, r, 0)),
        out_shape=jax.ShapeDtypeStruct((b, s, B_W), BF16),
        compiler_params=_cparams(("parallel", "arbitrary")), name="attn_b",
    )(qkv, qkv, qkv, bias, gout)


def _attn_c_kernel(q_ref, k_ref, v_ref, gout_ref, o_ref):
    q = q_ref[0]
    k = k_ref[0]
    v = v_ref[0]
    g = C_HEADS // C_KV_HEADS
    outs = []
    for hk in range(C_KV_HEADS):
        kh = k[:, hk * HEAD_DIM:(hk + 1) * HEAD_DIM]
        vh = v[:, hk * HEAD_DIM:(hk + 1) * HEAD_DIM]
        for gi in range(g):
            h = hk * g + gi
            logits = _dot_nt(q[:, h * HEAD_DIM:(h + 1) * HEAD_DIM], kh)
            m = jnp.max(logits, axis=-1, keepdims=True)
            p = jnp.exp(logits - m)
            denom = jnp.sum(p, axis=-1, keepdims=True)
            outs.append(jnp.dot(p.astype(BF16), vh, preferred_element_type=F32) / denom)
    o = jnp.concatenate(outs, axis=-1)
    o_ref[0] = _rms(o, gout_ref[...]).astype(o_ref.dtype)


def _attn_c(qkv, gout):
    b, s, _ = qkv.shape
    return pl.pallas_call(
        _attn_c_kernel, grid=(b, s // C_BLOCK),
        in_specs=[pl.BlockSpec((1, C_BLOCK, C_Q), lambda bi, j: (bi, j, _PERM_OFF["qc"] // C_Q)),
                  pl.BlockSpec((1, s, C_KV), lambda bi, j: (bi, 0, _PERM_OFF["kc"] // C_KV)),
                  pl.BlockSpec((1, s, C_KV), lambda bi, j: (bi, 0, _PERM_OFF["vc"] // C_KV)),
                  _resident((1, C_Q))],
        out_specs=pl.BlockSpec((1, C_BLOCK, C_Q), lambda bi, j: (bi, j, 0)),
        out_shape=jax.ShapeDtypeStruct((b, s, C_Q), BF16),
        compiler_params=_cparams(("parallel", "arbitrary")), name="attn_c",
    )(qkv, qkv, qkv, gout)


def _outproj_kernel(x_ref, a_ref, b_ref, c_ref, w_ref, o_ref):
    acc = jnp.dot(a_ref[...], w_ref[0:A_Q, :], preferred_element_type=F32)
    acc += jnp.dot(b_ref[...], w_ref[A_Q:A_Q + B_W, :], preferred_element_type=F32)
    acc += jnp.dot(c_ref[...], w_ref[A_Q + B_W:D_MIX, :], preferred_element_type=F32)
    o_ref[...] = x_ref[...] + acc


def _outproj(x, ma, mb, mc, w, tm=512):
    t, d = x.shape

    def row(n):
        return pl.BlockSpec((tm, n), lambda i: (i, 0))
    return pl.pallas_call(
        _outproj_kernel, grid=(t // tm,),
        in_specs=[row(d), row(A_Q), row(B_W), row(C_Q), _resident(w.shape)],
        out_specs=row(d), out_shape=jax.ShapeDtypeStruct((t, d), F32),
        compiler_params=_cparams(("parallel",)), name="outproj",
    )(x, ma, mb, mc, w)


def _memkv_kernel(mem_ref, g_ref, w_ref, gk_ref, k_ref, v_ref):
    mn = _rms(mem_ref[0], g_ref[...]).astype(BF16)
    kv = jnp.dot(mn, w_ref[...], preferred_element_type=F32)
    for h in range(MEM_HEADS):
        hs = slice(h * MEM_HEAD_DIM, (h + 1) * MEM_HEAD_DIM)
        k_ref[0, :, hs] = _rms(kv[:, hs], gk_ref[...]).astype(BF16)
    v_ref[0] = kv[:, MEM_W:].astype(BF16)


def _memkv(mem, g, w, gk):
    b, m, d = mem.shape
    out = pl.BlockSpec((1, m, MEM_W), lambda i: (i, 0, 0))
    return pl.pallas_call(
        _memkv_kernel, grid=(b,),
        in_specs=[pl.BlockSpec((1, m, d), lambda i: (i, 0, 0)), _resident((1, d)),
                  _resident(w.shape), _resident((1, MEM_HEAD_DIM))],
        out_specs=[out, out], out_shape=[jax.ShapeDtypeStruct((b, m, MEM_W), BF16)] * 2,
        compiler_params=_cparams(("parallel",)), name="memkv",
    )(mem, g, w, gk)


def _memattn_kernel(x_ref, g_ref, wq_ref, gq_ref, k_ref, v_ref, wo_ref, o_ref):
    x = x_ref[0]
    hn = _rms(x, g_ref[...]).astype(BF16)
    q = jnp.dot(hn, wq_ref[...], preferred_element_type=F32)
    outs = []
    for h in range(MEM_HEADS):
        hs = slice(h * MEM_HEAD_DIM, (h + 1) * MEM_HEAD_DIM)
        qh = _rms(q[:, hs], gq_ref[...]).astype(BF16)
        logits = _dot_nt(qh, k_ref[0, :, hs]) * (MEM_HEAD_DIM ** -0.5)
        m = jnp.max(logits, axis=-1, keepdims=True)
        p = jnp.exp(logits - m)
        denom = jnp.sum(p, axis=-1, keepdims=True)
        outs.append(jnp.dot(p.astype(BF16), v_ref[0, :, hs], preferred_element_type=F32) / denom)
    o = jnp.concatenate(outs, axis=-1).astype(BF16)
    o_ref[0] = x + jnp.dot(o, wo_ref[...], preferred_element_type=F32)


def _memattn(x, g, wq, gq, km, vm, wo, tm=512):
    b, s, d = x.shape
    m = km.shape[1]
    row = pl.BlockSpec((1, tm, d), lambda bi, i: (bi, i, 0))
    kv = pl.BlockSpec((1, m, MEM_W), lambda bi, i: (bi, 0, 0))
    return pl.pallas_call(
        _memattn_kernel, grid=(b, s // tm),
        in_specs=[row, _resident((1, d)), _resident(wq.shape), _resident((1, MEM_HEAD_DIM)), kv, kv,
                  _resident(wo.shape)],
        out_specs=row, out_shape=jax.ShapeDtypeStruct((b, s, d), F32),
        compiler_params=_cparams(("parallel", "arbitrary")), name="memattn",
    )(x, g, wq, gq, km, vm, wo)


def _extract_top(s, want_rank):
    cur = s
    rank = jnp.full(s.shape, float(PEER_NKEYS - 1), F32)
    tops = []
    for kk in range(PEER_TOPK):
        mk = jnp.max(cur, axis=0, keepdims=True)
        sel = cur == mk
        if want_rank:
            rank = jnp.where(sel, float(kk), rank)
        cur = jnp.where(sel, -jnp.inf, cur)
        tops.append(mk)
    return tops, rank


def _stack8(rows):
    sub = lax.broadcasted_iota(jnp.int32, (8, rows[0].shape[1]), 0)
    out = jnp.broadcast_to(rows[0], sub.shape)
    for kk in range(1, 8):
        out = jnp.where(sub == kk, rows[kk], out)
    return out


def _route_head(s1, s2):
    t1, _ = _extract_top(s1, False)
    t2, r2 = _extract_top(s2, True)
    ts2_lo = _stack8(t2[:8])
    ts2_hi = _stack8(t2[8:])
    sub = lax.broadcasted_iota(jnp.int32, ts2_lo.shape, 0)
    pieces = [t1[0] + ts2_lo, t1[0] + ts2_hi]
    for k1 in range(1, 8):
        pieces.append(jnp.where(sub < PEER_TOPK // (k1 + 1), t1[k1] + ts2_lo, -jnp.inf))
    pieces.append(_stack8(t1[8:]) + t2[0])
    cand = jnp.concatenate(pieces, axis=0)
    best = []
    for _ in range(PEER_TOPK):
        mk = jnp.max(cand, axis=0, keepdims=True)
        cand = jnp.where(cand == mk, -jnp.inf, cand)
        best.append(mk)
    tau = best[-1]
    top = t1[0] + t2[0]
    z = best[0] - top
    z = jnp.exp(z)
    for bk in best[1:]:
        z = z + jnp.exp(bk - top)
    n1 = jnp.zeros(s1.shape, F32)
    for k2 in range(PEER_TOPK):
        n1 = n1 + jnp.where(s1 + t2[k2] >= tau, 1.0, 0.0)
    a1 = jnp.exp(s1 - t1[0]) / z
    b2 = jnp.exp(s2 - t2[0])
    return a1, n1, b2, r2


def _router_kernel(x_ref, g_ref, wq_ref, keys_ref, hn_ref, a1_ref, n1_ref, b2_ref, r2_ref, q_scr,
                   *, tm):
    hn = _rms(x_ref[...], g_ref[...]).astype(BF16)
    hn_ref[...] = hn
    half = PEER_DKEY // 2
    for c in range(2 * PEER_HEADS):
        q_scr[c] = jnp.dot(hn, wq_ref[:, c * half:(c + 1) * half],
                           preferred_element_type=F32).astype(BF16)

    def head(h, carry):
        s1 = _dot_nt(keys_ref[2 * h], q_scr[2 * h])
        s2 = _dot_nt(keys_ref[2 * h + 1], q_scr[2 * h + 1])
        for lc in range(tm // LANE):
            ls = slice(lc * LANE, (lc + 1) * LANE)
            a1, n1, b2, r2 = _route_head(s1[:, ls], s2[:, ls])
            a1_ref[h, :, ls] = a1
            n1_ref[h, :, ls] = n1
            b2_ref[h, :, ls] = b2
            r2_ref[h, :, ls] = r2
        return carry
    lax.fori_loop(0, PEER_HEADS, head, 0)


def _router(x, g, wq, keys, tm=256):
    t, d = x.shape
    tab = pl.BlockSpec((PEER_HEADS, PEER_NKEYS, tm), lambda i: (0, 0, i))
    tab_shape = jax.ShapeDtypeStruct((PEER_HEADS, PEER_NKEYS, t), F32)
    return pl.pallas_call(
        functools.partial(_router_kernel, tm=tm), grid=(t // tm,),
        in_specs=[pl.BlockSpec((tm, d), lambda i: (i, 0)), _resident((1, d)), _resident(wq.shape),
                  _resident(keys.shape)],
        out_specs=[pl.BlockSpec((tm, d), lambda i: (i, 0)), tab, tab, tab, tab],
        out_shape=[jax.ShapeDtypeStruct((t, d), BF16)] + [tab_shape] * 4,
        scratch_shapes=[pltpu.VMEM((2 * PEER_HEADS, tm, PEER_DKEY // 2), BF16)],
        compiler_params=_cparams(("parallel",)), name="peer_router",
    )(x, g, wq, keys)


def _gelu(x):
    return 0.5 * x * (1.0 + lax.erf(x * (2.0 ** -0.5)))


def _peer_kernel(hn_ref, a1_ref, n1_ref, b2_ref, r2_ref, u_ref, vt_ref, y_ref, acc_ref, *, eb):
    e = pl.program_id(1)

    @pl.when(e == 0)
    def _():
        acc_ref[...] = jnp.zeros_like(acc_ref)

    pre = _dot_nt(u_ref[...], hn_ref[...])
    gates = []
    for il in range(eb // PEER_NKEYS):
        i = e * (eb // PEER_NKEYS) + il
        g = jnp.zeros((PEER_NKEYS, hn_ref.shape[0]), F32)
        for h in range(PEER_HEADS):
            n_i = n1_ref[h, pl.ds(i, 1), :]
            a_i = a1_ref[h, pl.ds(i, 1), :]
            g = g + jnp.where(r2_ref[h] < n_i, b2_ref[h], 0.0) * a_i
        gates.append(g)
    gate = jnp.concatenate(gates, axis=0) if len(gates) > 1 else gates[0]
    w = (_gelu(pre) * gate).astype(BF16)
    acc_ref[...] += jnp.dot(vt_ref[...], w, preferred_element_type=F32)

    @pl.when(e == pl.num_programs(1) - 1)
    def _():
        y_ref[...] = acc_ref[...].T


def _peer(hn, a1, n1, b2, r2, u, vt, tb=512, eb=256):
    t, d = hn.shape
    ne = u.shape[0]
    tab = pl.BlockSpec((PEER_HEADS, PEER_NKEYS, tb), lambda i, e: (0, 0, i))
    return pl.pallas_call(
        functools.partial(_peer_kernel, eb=eb), grid=(t // tb, ne // eb),
        in_specs=[pl.BlockSpec((tb, d), lambda i, e: (i, 0)), tab, tab, tab, tab,
                  pl.BlockSpec((eb, d), lambda i, e: (e, 0)),
                  pl.BlockSpec((d, eb), lambda i, e: (0, e))],
        out_specs=pl.BlockSpec((tb, d), lambda i, e: (i, 0)),
        out_shape=jax.ShapeDtypeStruct((t, d), F32),
        scratch_shapes=[pltpu.VMEM((d, tb), F32)],
        compiler_params=_cparams(("parallel", "arbitrary")), name="peer_experts",
    )(hn, a1, n1, b2, r2, u, vt)


def _add_kernel(a_ref, b_ref, o_ref):
    o_ref[...] = a_ref[...] + b_ref[...]


def _add(a, b, tm=512):
    t, d = a.shape
    row = pl.BlockSpec((tm, d), lambda i: (i, 0))
    return pl.pallas_call(_add_kernel, grid=(t // tm,), in_specs=[row, row], out_specs=row,
                          out_shape=jax.ShapeDtypeStruct((t, d), F32),
                          compiler_params=_cparams(("parallel",)), name="residual_add")(a, b)


def _t5_buckets(rel):
    nb = REL_BUCKETS // 2
    max_exact = nb // 2
    n = np.abs(rel)
    large = max_exact + (np.log(np.maximum(n, 1) / max_exact) / np.log(REL_MAX_DIST / max_exact)
                         * (nb - max_exact)).astype(np.int64)
    large = np.minimum(large, nb - 1)
    return np.where(rel > 0, nb, 0) + np.where(n < max_exact, n, large)


def _window_bias(rel_bias):
    span = A_BLOCK + 2 * A_WINDOW
    rel = (np.arange(span)[None, :] - A_WINDOW) - np.arange(A_BLOCK)[:, None]
    bias = jnp.transpose(rel_bias.astype(F32)[_t5_buckets(rel)], (2, 0, 1))
    return jnp.where(jnp.asarray(np.abs(rel) <= A_WINDOW)[None], bias, NEG)


def _neighbourhood_bias(rpb, rows):
    kh = min(NA_MAX_KH, rows)
    qc = np.arange(GRID_W)[:, None]
    kc = np.arange(GRID_W)[None, :]
    c_start = np.clip(qc - NA_KW // 2, 0, GRID_W - NA_KW)
    col_ok = (kc >= c_start) & (kc < c_start + NA_KW)
    dc = np.clip(kc - qc + NA_KW - 1, 0, 2 * NA_KW - 2)
    dr = np.arange(NA_MAX_KH)[:, None] + np.arange(kh)[None, :]
    dr = np.minimum(dr, 2 * NA_MAX_KH - 2)
    tab = rpb.astype(F32)[:, dr][:, :, :, dc]
    tab = jnp.where(jnp.asarray(col_ok)[None, None, None], tab, NEG)
    tab = jnp.transpose(tab, (0, 1, 3, 2, 4))
    return tab.reshape(B_HEADS, NA_MAX_KH, GRID_W, kh * GRID_W)


def _rope_tables(seq):
    half = HEAD_DIM // 2
    nf = half // 2
    inv = (ROPE_THETA ** (-np.arange(nf) * 2.0 / half)).astype(np.float32).astype(np.float64)
    t = np.arange(seq)
    pos = np.stack([t // GRID_W, t % GRID_W], axis=1).astype(np.float64)
    ang = pos[:, :, None] * inv[None, None, :]
    cos = np.repeat(np.cos(ang), 2, axis=1).reshape(seq, 2, 2, nf)
    sin = np.repeat(np.sin(ang), 2, axis=1).reshape(seq, 2, 2, nf)
    sin = sin * np.array([-1.0, 1.0])[None, None, :, None]
    cos = np.tile(cos.reshape(seq, HEAD_DIM), (1, CHUNK // HEAD_DIM))
    sin = np.tile(sin.reshape(seq, HEAD_DIM), (1, CHUNK // HEAD_DIM))
    return jnp.asarray(cos, F32), jnp.asarray(sin, F32)


def _permute_cols(w):
    return jnp.concatenate([w[..., _REF_SECTIONS[n][0]:_REF_SECTIONS[n][0] + _REF_SECTIONS[n][1]]
                            for n in _PERM_ORDER], axis=-1)


def _proj_gains(ga, gb, gc):
    scale = HEAD_DIM ** -0.5
    one = jnp.ones((HEAD_DIM,), F32)
    per = dict(qa=(ga[0] * scale, A_HEADS), ka=(ga[1], A_KV_HEADS), va=(one, A_KV_HEADS),
               qb=(gb[0] * scale, B_HEADS), kb=(gb[1], B_HEADS), vb=(one, B_HEADS),
               qc=(gc[0] * scale, C_HEADS), kc=(gc[1], C_KV_HEADS), vc=(one, C_KV_HEADS))
    return jnp.concatenate([jnp.tile(per[n][0].astype(F32), per[n][1]) for n in _PERM_ORDER])[None]


def kernel(x, mem, t5_rel_bias, norm_mix, w_in, qk_norm_a, sink_a, qk_norm_b, rpb_b, qk_norm_c,
           out_norm, w_out, norm_mem, norm_mem_kv, w_mem_q, w_mem_kv, qk_norm_mem, w_mem_o,
           norm_ffn, peer_w_q, peer_keys, peer_u, peer_v):
    b, s, d = x.shape
    depth = w_in.shape[0]
    t = b * s
    cos, sin = _rope_tables(s)
    blockdiag = jnp.asarray(np.kron(np.eye(CHUNK // HEAD_DIM), np.ones((HEAD_DIM, HEAD_DIM))), BF16)
    bias_a = _window_bias(t5_rel_bias)
    xf = x.reshape(t, d)
    y = None
    for l in range(depth):
        qkv, xf = _inproj(xf, y, norm_mix[l][None], _permute_cols(w_in[l]).astype(BF16),
                          _proj_gains(qk_norm_a[l], qk_norm_b[l], qk_norm_c[l]), cos, sin,
                          blockdiag, s)
        qkv = qkv.reshape(b, s, D_IN)
        g_out = out_norm[l]
        ma = _attn_a(qkv, sink_a[l].astype(F32), bias_a, g_out[None, :A_Q])
        mb = _attn_b(qkv, _neighbourhood_bias(rpb_b[l], s // GRID_W), g_out[None, A_Q:A_Q + B_W])
        mc = _attn_c(qkv, g_out[None, A_Q + B_W:])
        xf = _outproj(xf, ma.reshape(t, A_Q), mb.reshape(t, B_W), mc.reshape(t, C_Q),
                      w_out[l].astype(BF16))
        km, vm = _memkv(mem, norm_mem_kv[l][None], w_mem_kv[l].astype(BF16), qk_norm_mem[l, 1][None])
        xf = _memattn(xf.reshape(b, s, d), norm_mem[l][None], w_mem_q[l].astype(BF16),
                      qk_norm_mem[l, 0][None], km, vm, w_mem_o[l].astype(BF16)).reshape(t, d)
        keys = peer_keys[l].reshape(2 * PEER_HEADS, PEER_NKEYS, PEER_DKEY // 2).astype(BF16)
        hn, a1, n1, b2, r2 = _router(xf, norm_ffn[l][None], peer_w_q[l].astype(BF16), keys)
        y = _peer(hn, a1, n1, b2, r2, peer_u[l].astype(BF16), peer_v[l].astype(BF16).T)
    return _add(xf, y).reshape(b, s, d)
```

```python
import functools

import numpy as np
import jax
import jax.numpy as jnp
from jax import lax
from jax.experimental import pallas as pl
from jax.experimental.pallas import tpu as pltpu

F32 = jnp.float32
BF16 = jnp.bfloat16

GRID_W = 64
HEAD_DIM = 64
EPS = 1e-6
NEG = -1e30
A_HEADS, A_KV_HEADS, A_WINDOW, A_BLOCK = 12, 4, 128, 128
B_HEADS, NA_MAX_KH, NA_KW = 8, 8, 16
B_QROWS = 4
C_HEADS, C_KV_HEADS, C_BLOCK = 12, 4, 128
ROPE_THETA = 10000.0
REL_BUCKETS, REL_MAX_DIST = 32, 128
MEM_HEADS, MEM_HEAD_DIM = 4, 128
MEM_W = MEM_HEADS * MEM_HEAD_DIM
PEER_HEADS, PEER_NKEYS, PEER_DKEY, PEER_TOPK = 8, 128, 256, 16
A_Q, A_KV = A_HEADS * HEAD_DIM, A_KV_HEADS * HEAD_DIM
B_W = B_HEADS * HEAD_DIM
C_Q, C_KV = C_HEADS * HEAD_DIM, C_KV_HEADS * HEAD_DIM
D_MIX = A_Q + B_W + C_Q
D_IN = A_Q + 2 * A_KV + 3 * B_W + C_Q + 2 * C_KV

LANE = 128
BF16_ROWS = 16
CHUNK = 256
VMEM_LIMIT = 56 * 1024 * 1024

_REF_SECTIONS = dict(qa=(0, A_Q), ka=(A_Q, A_KV), va=(A_Q + A_KV, A_KV),
                     qb=(A_Q + 2 * A_KV, B_W), kb=(A_Q + 2 * A_KV + B_W, B_W),
                     vb=(A_Q + 2 * A_KV + 2 * B_W, B_W),
                     qc=(A_Q + 2 * A_KV + 3 * B_W, C_Q),
                     kc=(A_Q + 2 * A_KV + 3 * B_W + C_Q, C_KV),
                     vc=(A_Q + 2 * A_KV + 3 * B_W + C_Q + C_KV, C_KV))
_PERM_ORDER = ("qa", "qc", "qb", "kb", "vb", "ka", "va", "kc", "vc")
_KIND = dict(qa="norm", qc="rope", qb="norm", kb="norm", vb="plain", ka="norm", va="plain",
             kc="rope", vc="plain")
_PERM_OFF = {}
_off = 0
for _n in _PERM_ORDER:
    _PERM_OFF[_n] = _off
    _off += _REF_SECTIONS[_n][1]
_CHUNK_KINDS = []
for _n in _PERM_ORDER:
    _CHUNK_KINDS += [_KIND[_n]] * (_REF_SECTIONS[_n][1] // CHUNK)


def _cparams(sem):
    return pltpu.CompilerParams(dimension_semantics=sem, vmem_limit_bytes=VMEM_LIMIT)


def _resident(shape):
    nd = len(shape)
    return pl.BlockSpec(shape, lambda *_: (0,) * nd, pipeline_mode=pl.Buffered(1))


def _rms(x, g):
    return x * lax.rsqrt(jnp.mean(x * x, axis=-1, keepdims=True) + EPS) * g


def _dot_nt(a, b):
    return lax.dot_general(a, b, (((1,), (1,)), ((), ())), preferred_element_type=F32)


def _inproj_kernel(x_ref, g_ref, w_ref, gain_ref, cos_ref, sin_ref, bd_ref, qkv_ref):
    hn = _rms(x_ref[...], g_ref[...]).astype(BF16)
    lane = lax.broadcasted_iota(jnp.int32, (1, CHUNK), 1)
    lower_half = (lane % 32) < 16
    for c, kind in enumerate(_CHUNK_KINDS):
        cols = slice(c * CHUNK, (c + 1) * CHUNK)
        acc = jnp.dot(hn, w_ref[:, cols], preferred_element_type=F32)
        if kind != "plain":
            sq = acc * acc
            hi = sq.astype(BF16)
            lo = (sq - hi.astype(F32)).astype(BF16)
            ss = (jnp.dot(hi, bd_ref[...], preferred_element_type=F32)
                  + jnp.dot(lo, bd_ref[...], preferred_element_type=F32))
            acc = acc * lax.rsqrt(ss * (1.0 / HEAD_DIM) + EPS) * gain_ref[:, cols]
        if kind == "rope":
            partner = jnp.where(lower_half, pltpu.roll(acc, CHUNK - 16, 1), pltpu.roll(acc, 16, 1))
            acc = acc * cos_ref[...] + partner * sin_ref[...]
        qkv_ref[:, cols] = acc.astype(BF16)


def _inproj(x, g, w, gain, cos, sin, bd, seq, tm=512):
    t, d = x.shape
    n = w.shape[1]
    nseq = seq // tm
    tab = pl.BlockSpec((tm, CHUNK), lambda i: (i % nseq, 0))
    return pl.pallas_call(
        _inproj_kernel, grid=(t // tm,),
        in_specs=[pl.BlockSpec((tm, d), lambda i: (i, 0)), _resident((1, d)), _resident((d, n)),
                  _resident((1, n)), tab, tab, _resident((CHUNK, CHUNK))],
        out_specs=pl.BlockSpec((tm, n), lambda i: (i, 0)),
        out_shape=jax.ShapeDtypeStruct((t, n), BF16), compiler_params=_cparams(("parallel",)),
        name="inproj")(x, g, w, gain, cos, sin, bd)


def _attn_a_kernel(sink_ref, q_ref, kp_ref, kc_ref, kn_ref, vp_ref, vc_ref, vn_ref, bias_ref,
                   gout_ref, o_ref):
    j = pl.program_id(1)
    nb = pl.num_programs(1)
    q = q_ref[0]
    k = jnp.concatenate([kp_ref[0], kc_ref[0], kn_ref[0]], axis=0)
    v = jnp.concatenate([vp_ref[0], vc_ref[0], vn_ref[0]], axis=0)
    col = lax.broadcasted_iota(jnp.int32, (1, 3 * A_BLOCK), 1)
    first_ok = jnp.where(j > 0, 0, A_BLOCK)
    end_ok = jnp.where(j < nb - 1, 3 * A_BLOCK, 2 * A_BLOCK)
    ok = (col >= first_ok) & (col < end_ok)
    g = A_HEADS // A_KV_HEADS
    row = lax.broadcasted_iota(jnp.int32, (g * A_BLOCK, 1), 0)
    outs = []
    for hk in range(A_KV_HEADS):
        kh = k[:, hk * HEAD_DIM:(hk + 1) * HEAD_DIM]
        vh = v[:, hk * HEAD_DIM:(hk + 1) * HEAD_DIM]
        qs = jnp.concatenate([q[:, (hk * g + gi) * HEAD_DIM:(hk * g + gi + 1) * HEAD_DIM]
                              for gi in range(g)], axis=0)
        s = jnp.full((g * A_BLOCK, 1), sink_ref[hk * g], F32)
        for gi in range(1, g):
            s = jnp.where(row >= gi * A_BLOCK, sink_ref[hk * g + gi], s)
        logits = jnp.where(ok, _dot_nt(qs, kh) + bias_ref[hk], NEG)
        m = jnp.maximum(jnp.max(logits, axis=-1, keepdims=True), s)
        p = jnp.exp(logits - m)
        denom = jnp.sum(p, axis=-1, keepdims=True) + jnp.exp(s - m)
        o = jnp.dot(p.astype(BF16), vh, preferred_element_type=F32) / denom
        outs += [o[gi * A_BLOCK:(gi + 1) * A_BLOCK] for gi in range(g)]
    o = jnp.concatenate(outs, axis=-1)
    o_ref[0] = _rms(o, gout_ref[...]).astype(o_ref.dtype)


def _attn_a(qkv, sink, bias, gout):
    b, s, _ = qkv.shape
    nb = s // A_BLOCK
    kblk, vblk = _PERM_OFF["ka"] // A_KV, _PERM_OFF["va"] // A_KV
    qspec = pl.BlockSpec((1, A_BLOCK, A_Q), lambda bi, j: (bi, j, _PERM_OFF["qa"] // A_Q))

    def kv(blk, shift):
        return pl.BlockSpec((1, A_BLOCK, A_KV),
                            lambda bi, j: (bi, jnp.clip(j + shift, 0, nb - 1), blk))
    return pl.pallas_call(
        _attn_a_kernel, grid=(b, nb),
        in_specs=[pl.BlockSpec(memory_space=pltpu.SMEM), qspec,
                  kv(kblk, -1), kv(kblk, 0), kv(kblk, 1), kv(vblk, -1), kv(vblk, 0), kv(vblk, 1),
                  _resident(bias.shape), _resident((1, A_Q))],
        out_specs=pl.BlockSpec((1, A_BLOCK, A_Q), lambda bi, j: (bi, j, 0)),
        out_shape=jax.ShapeDtypeStruct((b, s, A_Q), BF16),
        compiler_params=_cparams(("parallel", "parallel")), name="attn_a",
    )(sink, qkv, qkv, qkv, qkv, qkv, qkv, qkv, bias, gout)


def _attn_b_kernel(ids_ref, q_ref, k_ref, v_ref, bias_ref, gout_ref, o_ref, *, rows, kh):
    del ids_ref
    m = pl.program_id(1)
    span = kh + B_QROWS
    first = jnp.clip(m * B_QROWS - kh // 2, 0, rows - span)
    start = pl.multiple_of(first * GRID_W, GRID_W)
    q = q_ref[0]
    k = k_ref[0, pl.ds(start, span * GRID_W), :]
    v = v_ref[0, pl.ds(start, span * GRID_W), :]
    outs = []
    for h in range(B_HEADS):
        hs = slice(h * HEAD_DIM, (h + 1) * HEAD_DIM)
        logits = _dot_nt(q[:, hs], k[:, hs]) + bias_ref[h, 0]
        m = jnp.max(logits, axis=-1, keepdims=True)
        p = jnp.exp(logits - m)
        denom = jnp.sum(p, axis=-1, keepdims=True)
        outs.append(jnp.dot(p.astype(BF16), v[:, hs], preferred_element_type=F32) / denom)
    o = jnp.concatenate(outs, axis=-1)
    o_ref[0] = _rms(o, gout_ref[...]).astype(o_ref.dtype)


def _attn_b(qkv, bias, bias_ids, gout):
    b, s, _ = qkv.shape
    rows = s // GRID_W
    kh = min(NA_MAX_KH, rows)
    nq, nk = B_QROWS * GRID_W, (kh + B_QROWS) * GRID_W
    grid_spec = pltpu.PrefetchScalarGridSpec(
        num_scalar_prefetch=1, grid=(b, rows // B_QROWS),
        in_specs=[pl.BlockSpec((1, nq, B_W), lambda bi, m, ids: (bi, m, _PERM_OFF["qb"] // B_W)),
                  pl.BlockSpec((1, s, B_W), lambda bi, m, ids: (bi, 0, _PERM_OFF["kb"] // B_W)),
                  pl.BlockSpec((1, s, B_W), lambda bi, m, ids: (bi, 0, _PERM_OFF["vb"] // B_W)),
                  pl.BlockSpec((B_HEADS, 1, nq, nk), lambda bi, m, ids: (0, ids[m], 0, 0)),
                  pl.BlockSpec((1, B_W), lambda bi, m, ids: (0, 0))],
        out_specs=pl.BlockSpec((1, nq, B_W), lambda bi, m, ids: (bi, m, 0)))
    return pl.pallas_call(
        functools.partial(_attn_b_kernel, rows=rows, kh=kh), grid_spec=grid_spec,
        out_shape=jax.ShapeDtypeStruct((b, s, B_W), BF16),
        compiler_params=_cparams(("parallel", "arbitrary")), name="attn_b",
    )(bias_ids, qkv, qkv, qkv, bias, gout)


def _attn_c_kernel(q_ref, k_ref, v_ref, gout_ref, o_ref):
    q = q_ref[0]
    k = k_ref[0]
    v = v_ref[0]
    g = C_HEADS // C_KV_HEADS
    outs = []
    for hk in range(C_KV_HEADS):
        kh = k[:, hk * HEAD_DIM:(hk + 1) * HEAD_DIM]
        vh = v[:, hk * HEAD_DIM:(hk + 1) * HEAD_DIM]
        qs = jnp.concatenate([q[:, (hk * g + gi) * HEAD_DIM:(hk * g + gi + 1) * HEAD_DIM]
                              for gi in range(g)], axis=0)
        logits = _dot_nt(qs, kh)
        m = jnp.max(logits, axis=-1, keepdims=True)
        p = jnp.exp(logits - m)
        denom = jnp.sum(p, axis=-1, keepdims=True)
        o = jnp.dot(p.astype(BF16), vh, preferred_element_type=F32) / denom
        outs += [o[gi * C_BLOCK:(gi + 1) * C_BLOCK] for gi in range(g)]
    o = jnp.concatenate(outs, axis=-1)
    o_ref[0] = _rms(o, gout_ref[...]).astype(o_ref.dtype)


def _attn_c(qkv, gout):
    b, s, _ = qkv.shape
    return pl.pallas_call(
        _attn_c_kernel, grid=(b, s // C_BLOCK),
        in_specs=[pl.BlockSpec((1, C_BLOCK, C_Q), lambda bi, j: (bi, j, _PERM_OFF["qc"] // C_Q)),
                  pl.BlockSpec((1, s, C_KV), lambda bi, j: (bi, 0, _PERM_OFF["kc"] // C_KV)),
                  pl.BlockSpec((1, s, C_KV), lambda bi, j: (bi, 0, _PERM_OFF["vc"] // C_KV)),
                  _resident((1, C_Q))],
        out_specs=pl.BlockSpec((1, C_BLOCK, C_Q), lambda bi, j: (bi, j, 0)),
        out_shape=jax.ShapeDtypeStruct((b, s, C_Q), BF16),
        compiler_params=_cparams(("parallel", "arbitrary")), name="attn_c",
    )(qkv, qkv, qkv, gout)


def _outproj_kernel(x_ref, a_ref, b_ref, c_ref, w_ref, o_ref):
    acc = jnp.dot(a_ref[...], w_ref[0:A_Q, :], preferred_element_type=F32)
    acc += jnp.dot(b_ref[...], w_ref[A_Q:A_Q + B_W, :], preferred_element_type=F32)
    acc += jnp.dot(c_ref[...], w_ref[A_Q + B_W:D_MIX, :], preferred_element_type=F32)
    o_ref[...] = x_ref[...] + acc


def _outproj(x, ma, mb, mc, w, tm=512):
    t, d = x.shape

    def row(n):
        return pl.BlockSpec((tm, n), lambda i: (i, 0))
    return pl.pallas_call(
        _outproj_kernel, grid=(t // tm,),
        in_specs=[row(d), row(A_Q), row(B_W), row(C_Q), _resident(w.shape)],
        out_specs=row(d), out_shape=jax.ShapeDtypeStruct((t, d), F32),
        compiler_params=_cparams(("parallel",)), name="outproj",
    )(x, ma, mb, mc, w)


def _memkv_kernel(mem_ref, g_ref, w_ref, gk_ref, k_ref, v_ref):
    mn = _rms(mem_ref[0], g_ref[...]).astype(BF16)
    kv = jnp.dot(mn, w_ref[...], preferred_element_type=F32)
    for h in range(MEM_HEADS):
        hs = slice(h * MEM_HEAD_DIM, (h + 1) * MEM_HEAD_DIM)
        k_ref[0, :, hs] = _rms(kv[:, hs], gk_ref[...]).astype(BF16)
    v_ref[0] = kv[:, MEM_W:].astype(BF16)


def _memkv(mem, g, w, gk):
    b, m, d = mem.shape
    out = pl.BlockSpec((1, m, MEM_W), lambda i: (i, 0, 0))
    return pl.pallas_call(
        _memkv_kernel, grid=(b,),
        in_specs=[pl.BlockSpec((1, m, d), lambda i: (i, 0, 0)), _resident((1, d)),
                  _resident(w.shape), _resident((1, MEM_HEAD_DIM))],
        out_specs=[out, out], out_shape=[jax.ShapeDtypeStruct((b, m, MEM_W), BF16)] * 2,
        compiler_params=_cparams(("parallel",)), name="memkv",
    )(mem, g, w, gk)


def _memattn_kernel(x_ref, g_ref, wq_ref, gq_ref, k_ref, v_ref, wo_ref, o_ref):
    x = x_ref[0]
    hn = _rms(x, g_ref[...]).astype(BF16)
    q = jnp.dot(hn, wq_ref[...], preferred_element_type=F32)
    outs = []
    for h in range(MEM_HEADS):
        hs = slice(h * MEM_HEAD_DIM, (h + 1) * MEM_HEAD_DIM)
        qh = _rms(q[:, hs], gq_ref[...]).astype(BF16)
        logits = _dot_nt(qh, k_ref[0, :, hs]) * (MEM_HEAD_DIM ** -0.5)
        m = jnp.max(logits, axis=-1, keepdims=True)
        p = jnp.exp(logits - m)
        denom = jnp.sum(p, axis=-1, keepdims=True)
        outs.append(jnp.dot(p.astype(BF16), v_ref[0, :, hs], preferred_element_type=F32) / denom)
    o = jnp.concatenate(outs, axis=-1).astype(BF16)
    o_ref[0] = x + jnp.dot(o, wo_ref[...], preferred_element_type=F32)


def _memattn(x, g, wq, gq, km, vm, wo, tm=512):
    b, s, d = x.shape
    m = km.shape[1]
    row = pl.BlockSpec((1, tm, d), lambda bi, i: (bi, i, 0))
    kv = pl.BlockSpec((1, m, MEM_W), lambda bi, i: (bi, 0, 0))
    return pl.pallas_call(
        _memattn_kernel, grid=(b, s // tm),
        in_specs=[row, _resident((1, d)), _resident(wq.shape), _resident((1, MEM_HEAD_DIM)), kv, kv,
                  _resident(wo.shape)],
        out_specs=row, out_shape=jax.ShapeDtypeStruct((b, s, d), F32),
        compiler_params=_cparams(("parallel", "arbitrary")), name="memattn",
    )(x, g, wq, gq, km, vm, wo)


def _extract_top(s, want_rank):
    cur = s
    rank = jnp.full(s.shape, float(PEER_NKEYS - 1), F32)
    tops = []
    for kk in range(PEER_TOPK):
        mk = jnp.max(cur, axis=0, keepdims=True)
        sel = cur == mk
        if want_rank:
            rank = jnp.where(sel, float(kk), rank)
        cur = jnp.where(sel, -jnp.inf, cur)
        tops.append(mk)
    return tops, rank


def _stack8(rows):
    sub = lax.broadcasted_iota(jnp.int32, (8, rows[0].shape[1]), 0)
    out = jnp.broadcast_to(rows[0], sub.shape)
    for kk in range(1, 8):
        out = jnp.where(sub == kk, rows[kk], out)
    return out


def _route_head(s1, s2):
    t1, _ = _extract_top(s1, False)
    t2, r2 = _extract_top(s2, True)
    ts2_lo = _stack8(t2[:8])
    ts2_hi = _stack8(t2[8:])
    sub = lax.broadcasted_iota(jnp.int32, ts2_lo.shape, 0)
    pieces = [t1[0] + ts2_lo, t1[0] + ts2_hi]
    for k1 in range(1, 8):
        pieces.append(jnp.where(sub < PEER_TOPK // (k1 + 1), t1[k1] + ts2_lo, -jnp.inf))
    pieces.append(_stack8(t1[8:]) + t2[0])
    cand = jnp.concatenate(pieces, axis=0)
    best = []
    for _ in range(PEER_TOPK):
        mk = jnp.max(cand, axis=0, keepdims=True)
        cand = jnp.where(cand == mk, -jnp.inf, cand)
        best.append(mk)
    tau = best[-1]
    top = t1[0] + t2[0]
    z = best[0] - top
    z = jnp.exp(z)
    for bk in best[1:]:
        z = z + jnp.exp(bk - top)
    n1 = jnp.zeros(s1.shape, F32)
    for k2 in range(PEER_TOPK):
        n1 = n1 + jnp.where(s1 + t2[k2] >= tau, 1.0, 0.0)
    a1 = jnp.exp(s1 - t1[0]) / z
    b2 = jnp.exp(s2 - t2[0])
    return a1, n1, b2, r2


def _router_kernel(x_ref, g_ref, wq_ref, keys_ref, hn_ref, a1_ref, n1_ref, b2_ref, r2_ref, q_scr,
                   *, tm):
    hn = _rms(x_ref[...], g_ref[...]).astype(BF16)
    hn_ref[...] = hn
    half = PEER_DKEY // 2
    wide = wq_ref.shape[1] // 2
    for part in range(2):
        q = jnp.dot(hn, wq_ref[:, part * wide:(part + 1) * wide], preferred_element_type=F32)
        for c in range(wide // half):
            q_scr[part * (wide // half) + c] = q[:, c * half:(c + 1) * half].astype(BF16)

    def head(h, carry):
        s1 = _dot_nt(keys_ref[2 * h], q_scr[2 * h])
        s2 = _dot_nt(keys_ref[2 * h + 1], q_scr[2 * h + 1])
        for lc in range(tm // LANE):
            ls = slice(lc * LANE, (lc + 1) * LANE)
            a1, n1, b2, r2 = _route_head(s1[:, ls], s2[:, ls])
            a1_ref[h, :, ls] = a1.astype(BF16)
            n1_ref[h, :, ls] = n1.astype(BF16)
            b2_ref[h, :, ls] = b2.astype(BF16)
            r2_ref[h, :, ls] = r2.astype(BF16)
        return carry
    lax.fori_loop(0, PEER_HEADS, head, 0)


def _router(x, g, wq, keys, tm=512):
    t, d = x.shape
    tab = pl.BlockSpec((PEER_HEADS, PEER_NKEYS, tm), lambda i: (0, 0, i))
    tab_shape = jax.ShapeDtypeStruct((PEER_HEADS, PEER_NKEYS, t), BF16)
    return pl.pallas_call(
        functools.partial(_router_kernel, tm=tm), grid=(t // tm,),
        in_specs=[pl.BlockSpec((tm, d), lambda i: (i, 0)), _resident((1, d)), _resident(wq.shape),
                  _resident(keys.shape)],
        out_specs=[pl.BlockSpec((tm, d), lambda i: (i, 0)), tab, tab, tab, tab],
        out_shape=[jax.ShapeDtypeStruct((t, d), BF16)] + [tab_shape] * 4,
        scratch_shapes=[pltpu.VMEM((2 * PEER_HEADS, tm, PEER_DKEY // 2), BF16)],
        compiler_params=_cparams(("parallel",)), name="peer_router",
    )(x, g, wq, keys)


def _gelu(x):
    return 0.5 * x * (1.0 + lax.erf(x * (2.0 ** -0.5)))


def _peer_kernel(x_ref, hn_ref, a1_ref, n1_ref, b2_ref, r2_ref, u_ref, vt_ref, y_ref, acc_ref, w_ref,
                 *, eb):
    e = pl.program_id(1)
    tb = hn_ref.shape[0]

    @pl.when(e == 0)
    def _():
        acc_ref[...] = jnp.zeros_like(acc_ref)

    pre = _dot_nt(u_ref[...], hn_ref[...])
    tiles = PEER_NKEYS // BF16_ROWS
    for il in range(eb // PEER_NKEYS):
        g = None
        for h in range(PEER_HEADS):
            term = jnp.where(r2_ref[h] < n1_ref[h, il][None], b2_ref[h], 0.0) * a1_ref[h, il][None]
            g = term if g is None else g + term
        act = _gelu(pre[il * PEER_NKEYS:(il + 1) * PEER_NKEYS, :]).astype(BF16)
        w_ref[il * tiles:(il + 1) * tiles] = act.reshape(tiles, BF16_ROWS, tb) * g
    acc_ref[...] += jnp.dot(vt_ref[...], w_ref[...].reshape(eb, tb), preferred_element_type=F32)

    @pl.when(e == pl.num_programs(1) - 1)
    def _():
        y_ref[...] = x_ref[...] + acc_ref[...].T


def _peer(x, hn, a1, n1, b2, r2, u, vt, tb=512, eb=1024):
    t, d = hn.shape
    ne = u.shape[0]
    tiles = PEER_NKEYS // BF16_ROWS
    once = dict(pipeline_mode=pl.Buffered(1))
    key1 = pl.BlockSpec((PEER_HEADS, eb // PEER_NKEYS, BF16_ROWS, tb), lambda i, e: (0, e, 0, i))
    key2 = pl.BlockSpec((PEER_HEADS, tiles, BF16_ROWS, tb), lambda i, e: (0, 0, 0, i), **once)
    return pl.pallas_call(
        functools.partial(_peer_kernel, eb=eb), grid=(t // tb, ne // eb),
        in_specs=[pl.BlockSpec((tb, d), lambda i, e: (i, 0), **once),
                  pl.BlockSpec((tb, d), lambda i, e: (i, 0), **once), key1, key1, key2, key2,
                  pl.BlockSpec((eb, d), lambda i, e: (e, 0)),
                  pl.BlockSpec((d, eb), lambda i, e: (0, e))],
        out_specs=pl.BlockSpec((tb, d), lambda i, e: (i, 0)),
        out_shape=jax.ShapeDtypeStruct((t, d), F32),
        scratch_shapes=[pltpu.VMEM((d, tb), F32), pltpu.VMEM((eb // BF16_ROWS, BF16_ROWS, tb), BF16)],
        compiler_params=_cparams(("parallel", "arbitrary")), name="peer_experts",
    )(x, hn, a1, n1, b2, r2, u, vt)


def _t5_buckets(rel):
    nb = REL_BUCKETS // 2
    max_exact = nb // 2
    n = np.abs(rel)
    large = max_exact + (np.log(np.maximum(n, 1) / max_exact) / np.log(REL_MAX_DIST / max_exact)
                         * (nb - max_exact)).astype(np.int64)
    large = np.minimum(large, nb - 1)
    return np.where(rel > 0, nb, 0) + np.where(n < max_exact, n, large)


def _window_bias(rel_bias):
    span = A_BLOCK + 2 * A_WINDOW
    rel = (np.arange(span)[None, :] - A_WINDOW) - np.arange(A_BLOCK)[:, None]
    bias = jnp.transpose(rel_bias.astype(F32)[_t5_buckets(rel)], (2, 0, 1))
    bias = jnp.where(jnp.asarray(np.abs(rel) <= A_WINDOW)[None], bias, NEG)
    return bias.reshape(A_KV_HEADS, (A_HEADS // A_KV_HEADS) * A_BLOCK, span)


def _neighbourhood_patterns(rows):
    kh = min(NA_MAX_KH, rows)
    span = kh + B_QROWS
    assert rows % B_QROWS == 0 and rows >= span
    m = np.arange(rows // B_QROWS)[:, None, None]
    q_row = m * B_QROWS + np.arange(B_QROWS)[None, :, None]
    k_row = np.clip(m * B_QROWS - kh // 2, 0, rows - span) + np.arange(span)[None, None, :]
    r0 = np.clip(q_row - kh // 2, 0, rows - kh)
    row_ok = (k_row >= r0) & (k_row < r0 + kh)
    dr = np.where(row_ok, k_row - q_row + NA_MAX_KH - 1, 0)
    assert dr.min() >= 0 and dr.max() <= 2 * NA_MAX_KH - 2
    key = np.concatenate([dr.reshape(len(m), -1), row_ok.reshape(len(m), -1)], axis=1)
    _, first_idx, ids = np.unique(key, axis=0, return_index=True, return_inverse=True)
    return ids.reshape(-1).astype(np.int32), dr[first_idx], row_ok[first_idx]


def _neighbourhood_bias(rpb, rows):
    ids, dr, row_ok = _neighbourhood_patterns(rows)
    qc = np.arange(GRID_W)[:, None]
    kc = np.arange(GRID_W)[None, :]
    c_start = np.clip(qc - NA_KW // 2, 0, GRID_W - NA_KW)
    col_ok = (kc >= c_start) & (kc < c_start + NA_KW)
    dc = np.clip(kc - qc + NA_KW - 1, 0, 2 * NA_KW - 2)
    tab = rpb.astype(F32)[:, dr][..., dc]
    ok = row_ok[:, :, :, None, None] & col_ok[None, None, None]
    tab = jnp.where(jnp.asarray(ok)[None], tab, NEG)
    tab = jnp.transpose(tab, (0, 1, 2, 4, 3, 5))
    npat, nqr, nkr = dr.shape
    return tab.reshape(B_HEADS, npat, nqr * GRID_W, nkr * GRID_W), jnp.asarray(ids)


def _rope_tables(seq):
    half = HEAD_DIM // 2
    nf = half // 2
    inv = (ROPE_THETA ** (-np.arange(nf) * 2.0 / half)).astype(np.float32).astype(np.float64)
    t = np.arange(seq)
    pos = np.stack([t // GRID_W, t % GRID_W], axis=1).astype(np.float64)
    ang = pos[:, :, None] * inv[None, None, :]
    cos = np.repeat(np.cos(ang), 2, axis=1).reshape(seq, 2, 2, nf)
    sin = np.repeat(np.sin(ang), 2, axis=1).reshape(seq, 2, 2, nf)
    sin = sin * np.array([-1.0, 1.0])[None, None, :, None]
    cos = np.tile(cos.reshape(seq, HEAD_DIM), (1, CHUNK // HEAD_DIM))
    sin = np.tile(sin.reshape(seq, HEAD_DIM), (1, CHUNK // HEAD_DIM))
    return jnp.asarray(cos, F32), jnp.asarray(sin, F32)


def _permute_cols(w):
    return jnp.concatenate([w[..., _REF_SECTIONS[n][0]:_REF_SECTIONS[n][0] + _REF_SECTIONS[n][1]]
                            for n in _PERM_ORDER], axis=-1)


def _proj_gains(ga, gb, gc):
    scale = HEAD_DIM ** -0.5
    one = jnp.ones((HEAD_DIM,), F32)
    per = dict(qa=(ga[0] * scale, A_HEADS), ka=(ga[1], A_KV_HEADS), va=(one, A_KV_HEADS),
               qb=(gb[0] * scale, B_HEADS), kb=(gb[1], B_HEADS), vb=(one, B_HEADS),
               qc=(gc[0] * scale, C_HEADS), kc=(gc[1], C_KV_HEADS), vc=(one, C_KV_HEADS))
    return jnp.concatenate([jnp.tile(per[n][0].astype(F32), per[n][1]) for n in _PERM_ORDER])[None]


def kernel(x, mem, t5_rel_bias, norm_mix, w_in, qk_norm_a, sink_a, qk_norm_b, rpb_b, qk_norm_c,
           out_norm, w_out, norm_mem, norm_mem_kv, w_mem_q, w_mem_kv, qk_norm_mem, w_mem_o,
           norm_ffn, peer_w_q, peer_keys, peer_u, peer_v):
    b, s, d = x.shape
    depth = w_in.shape[0]
    t = b * s
    cos, sin = _rope_tables(s)
    blockdiag = jnp.asarray(np.kron(np.eye(CHUNK // HEAD_DIM), np.ones((HEAD_DIM, HEAD_DIM))), BF16)
    bias_a = _window_bias(t5_rel_bias)
    xf = x.reshape(t, d)
    for l in range(depth):
        qkv = _inproj(xf, norm_mix[l][None], _permute_cols(w_in[l]).astype(BF16),
                      _proj_gains(qk_norm_a[l], qk_norm_b[l], qk_norm_c[l]), cos, sin,
                      blockdiag, s).reshape(b, s, D_IN)
        g_out = out_norm[l]
        ma = _attn_a(qkv, sink_a[l].astype(F32), bias_a, g_out[None, :A_Q])
        bias_b, ids_b = _neighbourhood_bias(rpb_b[l], s // GRID_W)
        mb = _attn_b(qkv, bias_b, ids_b, g_out[None, A_Q:A_Q + B_W])
        mc = _attn_c(qkv, g_out[None, A_Q + B_W:])
        xf = _outproj(xf, ma.reshape(t, A_Q), mb.reshape(t, B_W), mc.reshape(t, C_Q),
                      w_out[l].astype(BF16))
        km, vm = _memkv(mem, norm_mem_kv[l][None], w_mem_kv[l].astype(BF16), qk_norm_mem[l, 1][None])
        xf = _memattn(xf.reshape(b, s, d), norm_mem[l][None], w_mem_q[l].astype(BF16),
                      qk_norm_mem[l, 0][None], km, vm, w_mem_o[l].astype(BF16)).reshape(t, d)
        keys = peer_keys[l].reshape(2 * PEER_HEADS, PEER_NKEYS, PEER_DKEY // 2).astype(BF16)
        hn, a1, n1, b2, r2 = _router(xf, norm_ffn[l][None], peer_w_q[l].astype(BF16), keys)
        rep = (PEER_HEADS, PEER_NKEYS, BF16_ROWS, t)
        fold = (PEER_HEADS, PEER_NKEYS // BF16_ROWS, BF16_ROWS, t)
        xf = _peer(xf, hn, jnp.broadcast_to(a1[:, :, None, :], rep),
                   jnp.broadcast_to(n1[:, :, None, :], rep), b2.reshape(fold), r2.reshape(fold),
                   peer_u[l].astype(BF16), peer_v[l].astype(BF16).T)
    return xf.reshape(b, s, d)
```

```python
import functools

import numpy as np
import jax
import jax.numpy as jnp
from jax import lax
from jax.experimental import pallas as pl
from jax.experimental.pallas import tpu as pltpu

F32 = jnp.float32
BF16 = jnp.bfloat16

GRID_W = 64
HEAD_DIM = 64
EPS = 1e-6
NEG = -1e30
A_HEADS, A_KV_HEADS, A_WINDOW, A_BLOCK = 12, 4, 128, 128
B_HEADS, NA_MAX_KH, NA_KW = 8, 8, 16
B_QROWS = 4
C_HEADS, C_KV_HEADS, C_BLOCK = 12, 4, 128
ROPE_THETA = 10000.0
REL_BUCKETS, REL_MAX_DIST = 32, 128
MEM_HEADS, MEM_HEAD_DIM = 4, 128
MEM_W = MEM_HEADS * MEM_HEAD_DIM
PEER_HEADS, PEER_NKEYS, PEER_DKEY, PEER_TOPK = 8, 128, 256, 16
A_Q, A_KV = A_HEADS * HEAD_DIM, A_KV_HEADS * HEAD_DIM
B_W = B_HEADS * HEAD_DIM
C_Q, C_KV = C_HEADS * HEAD_DIM, C_KV_HEADS * HEAD_DIM
D_MIX = A_Q + B_W + C_Q
D_IN = A_Q + 2 * A_KV + 3 * B_W + C_Q + 2 * C_KV

LANE = 128
BF16_ROWS = 16
CHUNK = 256
VMEM_LIMIT = 56 * 1024 * 1024

_REF_SECTIONS = dict(qa=(0, A_Q), ka=(A_Q, A_KV), va=(A_Q + A_KV, A_KV),
                     qb=(A_Q + 2 * A_KV, B_W), kb=(A_Q + 2 * A_KV + B_W, B_W),
                     vb=(A_Q + 2 * A_KV + 2 * B_W, B_W),
                     qc=(A_Q + 2 * A_KV + 3 * B_W, C_Q),
                     kc=(A_Q + 2 * A_KV + 3 * B_W + C_Q, C_KV),
                     vc=(A_Q + 2 * A_KV + 3 * B_W + C_Q + C_KV, C_KV))
_PERM_ORDER = ("qa", "qc", "qb", "kb", "vb", "ka", "va", "kc", "vc")
_KIND = dict(qa="norm", qc="rope", qb="norm", kb="norm", vb="plain", ka="norm", va="plain",
             kc="rope", vc="plain")
_PERM_OFF = {}
_off = 0
for _n in _PERM_ORDER:
    _PERM_OFF[_n] = _off
    _off += _REF_SECTIONS[_n][1]
_CHUNK_KINDS = []
for _n in _PERM_ORDER:
    _CHUNK_KINDS += [_KIND[_n]] * (_REF_SECTIONS[_n][1] // CHUNK)


def _cparams(sem):
    return pltpu.CompilerParams(dimension_semantics=sem, vmem_limit_bytes=VMEM_LIMIT)


def _resident(shape):
    nd = len(shape)
    return pl.BlockSpec(shape, lambda *_: (0,) * nd, pipeline_mode=pl.Buffered(1))


def _rms(x, g):
    return x * lax.rsqrt(jnp.mean(x * x, axis=-1, keepdims=True) + EPS) * g


def _dot_nt(a, b):
    return lax.dot_general(a, b, (((1,), (1,)), ((), ())), preferred_element_type=F32)


def _inproj_kernel(x_ref, g_ref, w_ref, gain_ref, cos_ref, sin_ref, bd_ref, qkv_ref):
    hn = _rms(x_ref[...], g_ref[...]).astype(BF16)
    lane = lax.broadcasted_iota(jnp.int32, (1, CHUNK), 1)
    lower_half = (lane % 32) < 16
    for c, kind in enumerate(_CHUNK_KINDS):
        cols = slice(c * CHUNK, (c + 1) * CHUNK)
        acc = jnp.dot(hn, w_ref[:, cols], preferred_element_type=F32)
        if kind != "plain":
            ss = jnp.dot((acc * acc).astype(BF16), bd_ref[...], preferred_element_type=F32)
            acc = acc * lax.rsqrt(ss * (1.0 / HEAD_DIM) + EPS) * gain_ref[:, cols]
        if kind == "rope":
            partner = jnp.where(lower_half, pltpu.roll(acc, CHUNK - 16, 1), pltpu.roll(acc, 16, 1))
            acc = acc * cos_ref[...] + partner * sin_ref[...]
        qkv_ref[:, cols] = acc.astype(BF16)


def _inproj(x, g, w, gain, cos, sin, bd, seq, tm=512):
    t, d = x.shape
    n = w.shape[1]
    nseq = seq // tm
    tab = pl.BlockSpec((tm, CHUNK), lambda i: (i % nseq, 0))
    return pl.pallas_call(
        _inproj_kernel, grid=(t // tm,),
        in_specs=[pl.BlockSpec((tm, d), lambda i: (i, 0)), _resident((1, d)), _resident((d, n)),
                  _resident((1, n)), tab, tab, _resident((CHUNK, CHUNK))],
        out_specs=pl.BlockSpec((tm, n), lambda i: (i, 0)),
        out_shape=jax.ShapeDtypeStruct((t, n), BF16), compiler_params=_cparams(("parallel",)),
        name="inproj")(x, g, w, gain, cos, sin, bd)


def _attn_a_kernel(sink_ref, q_ref, kp_ref, kc_ref, kn_ref, vp_ref, vc_ref, vn_ref, bias_ref,
                   gout_ref, o_ref):
    j = pl.program_id(1)
    nb = pl.num_programs(1)
    q = q_ref[0]
    k = jnp.concatenate([kp_ref[0], kc_ref[0], kn_ref[0]], axis=0)
    v = jnp.concatenate([vp_ref[0], vc_ref[0], vn_ref[0]], axis=0)
    col = lax.broadcasted_iota(jnp.int32, (1, 3 * A_BLOCK), 1)
    first_ok = jnp.where(j > 0, 0, A_BLOCK)
    end_ok = jnp.where(j < nb - 1, 3 * A_BLOCK, 2 * A_BLOCK)
    ok = (col >= first_ok) & (col < end_ok)
    g = A_HEADS // A_KV_HEADS
    row = lax.broadcasted_iota(jnp.int32, (g * A_BLOCK, 1), 0)
    outs = []
    for hk in range(A_KV_HEADS):
        kh = k[:, hk * HEAD_DIM:(hk + 1) * HEAD_DIM]
        vh = v[:, hk * HEAD_DIM:(hk + 1) * HEAD_DIM]
        qs = jnp.concatenate([q[:, (hk * g + gi) * HEAD_DIM:(hk * g + gi + 1) * HEAD_DIM]
                              for gi in range(g)], axis=0)
        s = jnp.full((g * A_BLOCK, 1), sink_ref[hk * g], F32)
        for gi in range(1, g):
            s = jnp.where(row >= gi * A_BLOCK, sink_ref[hk * g + gi], s)
        logits = jnp.where(ok, _dot_nt(qs, kh) + bias_ref[hk], NEG)
        m = jnp.maximum(jnp.max(logits, axis=-1, keepdims=True), s)
        p = jnp.exp(logits - m)
        denom = jnp.sum(p, axis=-1, keepdims=True) + jnp.exp(s - m)
        o = jnp.dot(p.astype(BF16), vh, preferred_element_type=F32) / denom
        outs += [o[gi * A_BLOCK:(gi + 1) * A_BLOCK] for gi in range(g)]
    o = jnp.concatenate(outs, axis=-1)
    o_ref[0] = _rms(o, gout_ref[...]).astype(o_ref.dtype)


def _attn_a(qkv, sink, bias, gout):
    b, s, _ = qkv.shape
    nb = s // A_BLOCK
    kblk, vblk = _PERM_OFF["ka"] // A_KV, _PERM_OFF["va"] // A_KV
    qspec = pl.BlockSpec((1, A_BLOCK, A_Q), lambda bi, j: (bi, j, _PERM_OFF["qa"] // A_Q))

    def kv(blk, shift):
        return pl.BlockSpec((1, A_BLOCK, A_KV),
                            lambda bi, j: (bi, jnp.clip(j + shift, 0, nb - 1), blk))
    return pl.pallas_call(
        _attn_a_kernel, grid=(b, nb),
        in_specs=[pl.BlockSpec(memory_space=pltpu.SMEM), qspec,
                  kv(kblk, -1), kv(kblk, 0), kv(kblk, 1), kv(vblk, -1), kv(vblk, 0), kv(vblk, 1),
                  _resident(bias.shape), _resident((1, A_Q))],
        out_specs=pl.BlockSpec((1, A_BLOCK, A_Q), lambda bi, j: (bi, j, 0)),
        out_shape=jax.ShapeDtypeStruct((b, s, A_Q), BF16),
        compiler_params=_cparams(("parallel", "parallel")), name="attn_a",
    )(sink, qkv, qkv, qkv, qkv, qkv, qkv, qkv, bias, gout)


def _attn_b_kernel(ids_ref, q_ref, k_ref, v_ref, bias_ref, gout_ref, o_ref, *, rows, kh):
    del ids_ref
    m = pl.program_id(1)
    span = kh + B_QROWS
    first = jnp.clip(m * B_QROWS - kh // 2, 0, rows - span)
    start = pl.multiple_of(first * GRID_W, GRID_W)
    q = q_ref[0]
    k = k_ref[0, pl.ds(start, span * GRID_W), :]
    v = v_ref[0, pl.ds(start, span * GRID_W), :]
    outs = []
    for h in range(B_HEADS):
        hs = slice(h * HEAD_DIM, (h + 1) * HEAD_DIM)
        logits = _dot_nt(q[:, hs], k[:, hs]) + bias_ref[h, 0]
        m = jnp.max(logits, axis=-1, keepdims=True)
        p = jnp.exp(logits - m)
        denom = jnp.sum(p, axis=-1, keepdims=True)
        outs.append(jnp.dot(p.astype(BF16), v[:, hs], preferred_element_type=F32) / denom)
    o = jnp.concatenate(outs, axis=-1)
    o_ref[0] = _rms(o, gout_ref[...]).astype(o_ref.dtype)


def _attn_b(qkv, bias, bias_ids, gout):
    b, s, _ = qkv.shape
    rows = s // GRID_W
    kh = min(NA_MAX_KH, rows)
    nq, nk = B_QROWS * GRID_W, (kh + B_QROWS) * GRID_W
    grid_spec = pltpu.PrefetchScalarGridSpec(
        num_scalar_prefetch=1, grid=(b, rows // B_QROWS),
        in_specs=[pl.BlockSpec((1, nq, B_W), lambda bi, m, ids: (bi, m, _PERM_OFF["qb"] // B_W)),
                  pl.BlockSpec((1, s, B_W), lambda bi, m, ids: (bi, 0, _PERM_OFF["kb"] // B_W)),
                  pl.BlockSpec((1, s, B_W), lambda bi, m, ids: (bi, 0, _PERM_OFF["vb"] // B_W)),
                  pl.BlockSpec((B_HEADS, 1, nq, nk), lambda bi, m, ids: (0, ids[m], 0, 0)),
                  pl.BlockSpec((1, B_W), lambda bi, m, ids: (0, 0))],
        out_specs=pl.BlockSpec((1, nq, B_W), lambda bi, m, ids: (bi, m, 0)))
    return pl.pallas_call(
        functools.partial(_attn_b_kernel, rows=rows, kh=kh), grid_spec=grid_spec,
        out_shape=jax.ShapeDtypeStruct((b, s, B_W), BF16),
        compiler_params=_cparams(("parallel", "arbitrary")), name="attn_b",
    )(bias_ids, qkv, qkv, qkv, bias, gout)


def _attn_c_kernel(q_ref, k_ref, v_ref, gout_ref, o_ref):
    q = q_ref[0]
    k = k_ref[0]
    v = v_ref[0]
    g = C_HEADS // C_KV_HEADS
    outs = []
    for hk in range(C_KV_HEADS):
        kh = k[:, hk * HEAD_DIM:(hk + 1) * HEAD_DIM]
        vh = v[:, hk * HEAD_DIM:(hk + 1) * HEAD_DIM]
        qs = jnp.concatenate([q[:, (hk * g + gi) * HEAD_DIM:(hk * g + gi + 1) * HEAD_DIM]
                              for gi in range(g)], axis=0)
        logits = _dot_nt(qs, kh)
        m = jnp.max(logits, axis=-1, keepdims=True)
        p = jnp.exp(logits - m)
        denom = jnp.sum(p, axis=-1, keepdims=True)
        o = jnp.dot(p.astype(BF16), vh, preferred_element_type=F32) / denom
        outs += [o[gi * C_BLOCK:(gi + 1) * C_BLOCK] for gi in range(g)]
    o = jnp.concatenate(outs, axis=-1)
    o_ref[0] = _rms(o, gout_ref[...]).astype(o_ref.dtype)


def _attn_c(qkv, gout):
    b, s, _ = qkv.shape
    return pl.pallas_call(
        _attn_c_kernel, grid=(b, s // C_BLOCK),
        in_specs=[pl.BlockSpec((1, C_BLOCK, C_Q), lambda bi, j: (bi, j, _PERM_OFF["qc"] // C_Q)),
                  pl.BlockSpec((1, s, C_KV), lambda bi, j: (bi, 0, _PERM_OFF["kc"] // C_KV)),
                  pl.BlockSpec((1, s, C_KV), lambda bi, j: (bi, 0, _PERM_OFF["vc"] // C_KV)),
                  _resident((1, C_Q))],
        out_specs=pl.BlockSpec((1, C_BLOCK, C_Q), lambda bi, j: (bi, j, 0)),
        out_shape=jax.ShapeDtypeStruct((b, s, C_Q), BF16),
        compiler_params=_cparams(("parallel", "arbitrary")), name="attn_c",
    )(qkv, qkv, qkv, gout)


def _outproj_kernel(x_ref, a_ref, b_ref, c_ref, w_ref, o_ref):
    acc = jnp.dot(a_ref[...], w_ref[0:A_Q, :], preferred_element_type=F32)
    acc += jnp.dot(b_ref[...], w_ref[A_Q:A_Q + B_W, :], preferred_element_type=F32)
    acc += jnp.dot(c_ref[...], w_ref[A_Q + B_W:D_MIX, :], preferred_element_type=F32)
    o_ref[...] = x_ref[...] + acc


def _outproj(x, ma, mb, mc, w, tm=512):
    t, d = x.shape

    def row(n):
        return pl.BlockSpec((tm, n), lambda i: (i, 0))
    return pl.pallas_call(
        _outproj_kernel, grid=(t // tm,),
        in_specs=[row(d), row(A_Q), row(B_W), row(C_Q), _resident(w.shape)],
        out_specs=row(d), out_shape=jax.ShapeDtypeStruct((t, d), F32),
        compiler_params=_cparams(("parallel",)), name="outproj",
    )(x, ma, mb, mc, w)


def _memkv_kernel(mem_ref, g_ref, w_ref, gk_ref, k_ref, v_ref):
    mn = _rms(mem_ref[0], g_ref[...]).astype(BF16)
    kv = jnp.dot(mn, w_ref[...], preferred_element_type=F32)
    for h in range(MEM_HEADS):
        hs = slice(h * MEM_HEAD_DIM, (h + 1) * MEM_HEAD_DIM)
        k_ref[0, :, hs] = _rms(kv[:, hs], gk_ref[...]).astype(BF16)
    v_ref[0] = kv[:, MEM_W:].astype(BF16)


def _memkv(mem, g, w, gk):
    b, m, d = mem.shape
    out = pl.BlockSpec((1, m, MEM_W), lambda i: (i, 0, 0))
    return pl.pallas_call(
        _memkv_kernel, grid=(b,),
        in_specs=[pl.BlockSpec((1, m, d), lambda i: (i, 0, 0)), _resident((1, d)),
                  _resident(w.shape), _resident((1, MEM_HEAD_DIM))],
        out_specs=[out, out], out_shape=[jax.ShapeDtypeStruct((b, m, MEM_W), BF16)] * 2,
        compiler_params=_cparams(("parallel",)), name="memkv",
    )(mem, g, w, gk)


def _memattn_kernel(x_ref, g_ref, wq_ref, gq_ref, k_ref, v_ref, wo_ref, o_ref):
    x = x_ref[0]
    hn = _rms(x, g_ref[...]).astype(BF16)
    q = jnp.dot(hn, wq_ref[...], preferred_element_type=F32)
    outs = []
    for h in range(MEM_HEADS):
        hs = slice(h * MEM_HEAD_DIM, (h + 1) * MEM_HEAD_DIM)
        qh = _rms(q[:, hs], gq_ref[...]).astype(BF16)
        logits = _dot_nt(qh, k_ref[0, :, hs]) * (MEM_HEAD_DIM ** -0.5)
        m = jnp.max(logits, axis=-1, keepdims=True)
        p = jnp.exp(logits - m)
        denom = jnp.sum(p, axis=-1, keepdims=True)
        outs.append(jnp.dot(p.astype(BF16), v_ref[0, :, hs], preferred_element_type=F32) / denom)
    o = jnp.concatenate(outs, axis=-1).astype(BF16)
    o_ref[0] = x + jnp.dot(o, wo_ref[...], preferred_element_type=F32)


def _memattn(x, g, wq, gq, km, vm, wo, tm=512):
    b, s, d = x.shape
    m = km.shape[1]
    row = pl.BlockSpec((1, tm, d), lambda bi, i: (bi, i, 0))
    kv = pl.BlockSpec((1, m, MEM_W), lambda bi, i: (bi, 0, 0))
    return pl.pallas_call(
        _memattn_kernel, grid=(b, s // tm),
        in_specs=[row, _resident((1, d)), _resident(wq.shape), _resident((1, MEM_HEAD_DIM)), kv, kv,
                  _resident(wo.shape)],
        out_specs=row, out_shape=jax.ShapeDtypeStruct((b, s, d), F32),
        compiler_params=_cparams(("parallel", "arbitrary")), name="memattn",
    )(x, g, wq, gq, km, vm, wo)


def _extract_top(s, want_rank):
    cur = s
    rank = jnp.full(s.shape, float(PEER_NKEYS - 1), F32)
    tops = []
    for kk in range(PEER_TOPK):
        mk = jnp.max(cur, axis=0, keepdims=True)
        sel = cur == mk
        if want_rank:
            rank = jnp.where(sel, float(kk), rank)
        cur = jnp.where(sel, -jnp.inf, cur)
        tops.append(mk)
    return tops, rank


def _stack8(rows):
    sub = lax.broadcasted_iota(jnp.int32, (8, rows[0].shape[1]), 0)
    out = jnp.broadcast_to(rows[0], sub.shape)
    for kk in range(1, 8):
        out = jnp.where(sub == kk, rows[kk], out)
    return out


def _route_head(s1, s2):
    t1, _ = _extract_top(s1, False)
    t2, r2 = _extract_top(s2, True)
    ts2_lo = _stack8(t2[:8])
    ts2_hi = _stack8(t2[8:])
    sub = lax.broadcasted_iota(jnp.int32, ts2_lo.shape, 0)
    pieces = [t1[0] + ts2_lo, t1[0] + ts2_hi]
    for k1 in range(1, 8):
        pieces.append(jnp.where(sub < PEER_TOPK // (k1 + 1), t1[k1] + ts2_lo, -jnp.inf))
    pieces.append(_stack8(t1[8:]) + t2[0])
    cand = jnp.concatenate(pieces, axis=0)
    best = []
    for _ in range(PEER_TOPK):
        mk = jnp.max(cand, axis=0, keepdims=True)
        cand = jnp.where(cand == mk, -jnp.inf, cand)
        best.append(mk)
    tau = best[-1]
    top = t1[0] + t2[0]
    z = best[0] - top
    z = jnp.exp(z)
    for bk in best[1:]:
        z = z + jnp.exp(bk - top)
    n1 = jnp.zeros(s1.shape, F32)
    for k2 in range(PEER_TOPK):
        n1 = n1 + jnp.where(s1 + t2[k2] >= tau, 1.0, 0.0)
    a1 = jnp.exp(s1 - t1[0]) / z
    b2 = jnp.exp(s2 - t2[0])
    return a1, n1, b2, r2


def _router_kernel(x_ref, g_ref, wq_ref, keys_ref, hn_ref, a1_ref, n1_ref, b2_ref, r2_ref, q_scr,
                   *, tm):
    hn = _rms(x_ref[...], g_ref[...]).astype(BF16)
    hn_ref[...] = hn
    half = PEER_DKEY // 2
    wide = wq_ref.shape[1] // 2
    for part in range(2):
        q = jnp.dot(hn, wq_ref[:, part * wide:(part + 1) * wide], preferred_element_type=F32)
        for c in range(wide // half):
            q_scr[part * (wide // half) + c] = q[:, c * half:(c + 1) * half].astype(BF16)

    def head(h, carry):
        s1 = _dot_nt(keys_ref[2 * h], q_scr[2 * h])
        s2 = _dot_nt(keys_ref[2 * h + 1], q_scr[2 * h + 1])
        for lc in range(tm // LANE):
            ls = slice(lc * LANE, (lc + 1) * LANE)
            a1, n1, b2, r2 = _route_head(s1[:, ls], s2[:, ls])
            a1_ref[h, :, ls] = a1
            n1_ref[h, :, ls] = n1
            b2_ref[h, :, ls] = b2.astype(BF16)
            r2_ref[h, :, ls] = r2.astype(BF16)
        return carry
    lax.fori_loop(0, PEER_HEADS, head, 0)


def _router(x, g, wq, keys, tm=512):
    t, d = x.shape
    tab = pl.BlockSpec((PEER_HEADS, PEER_NKEYS, tm), lambda i: (0, 0, i))
    tab_shape = (PEER_HEADS, PEER_NKEYS, t)
    return pl.pallas_call(
        functools.partial(_router_kernel, tm=tm), grid=(t // tm,),
        in_specs=[pl.BlockSpec((tm, d), lambda i: (i, 0)), _resident((1, d)), _resident(wq.shape),
                  _resident(keys.shape)],
        out_specs=[pl.BlockSpec((tm, d), lambda i: (i, 0)), tab, tab, tab, tab],
        out_shape=[jax.ShapeDtypeStruct((t, d), BF16)]
        + [jax.ShapeDtypeStruct(tab_shape, dt) for dt in (F32, F32, BF16, BF16)],
        scratch_shapes=[pltpu.VMEM((2 * PEER_HEADS, tm, PEER_DKEY // 2), BF16)],
        compiler_params=_cparams(("parallel",)), name="peer_router",
    )(x, g, wq, keys)


def _gelu(x):
    return 0.5 * x * (1.0 + lax.erf(x * (2.0 ** -0.5)))


def _peer_kernel(x_ref, hn_ref, a1_ref, n1_ref, b2_ref, r2_ref, u_ref, vt_ref, y_ref, acc_ref, w_ref,
                 *, eb):
    e = pl.program_id(1)
    tb = hn_ref.shape[0]

    @pl.when(e == 0)
    def _():
        acc_ref[...] = jnp.zeros_like(acc_ref)

    pre = _dot_nt(u_ref[...], hn_ref[...])
    tiles = PEER_NKEYS // BF16_ROWS
    for il in range(eb // PEER_NKEYS):
        g = None
        for h in range(PEER_HEADS):
            n_i = jnp.broadcast_to(n1_ref[h, il:il + 1, :], (BF16_ROWS, tb)).astype(BF16)[None]
            a_i = jnp.broadcast_to(a1_ref[h, il:il + 1, :], (BF16_ROWS, tb)).astype(BF16)[None]
            term = jnp.where(r2_ref[h] < n_i, b2_ref[h], 0.0) * a_i
            g = term if g is None else g + term
        act = _gelu(pre[il * PEER_NKEYS:(il + 1) * PEER_NKEYS, :]).astype(BF16)
        w_ref[il * tiles:(il + 1) * tiles] = act.reshape(tiles, BF16_ROWS, tb) * g
    acc_ref[...] += jnp.dot(vt_ref[...], w_ref[...].reshape(eb, tb), preferred_element_type=F32)

    @pl.when(e == pl.num_programs(1) - 1)
    def _():
        y_ref[...] = x_ref[...] + acc_ref[...].T


def _peer(x, hn, a1, n1, b2, r2, u, vt, tb=512, eb=1024):
    t, d = hn.shape
    nblk = u.shape[0] // eb
    tiles = PEER_NKEYS // BF16_ROWS
    once = dict(pipeline_mode=pl.Buffered(1))
    key1 = pl.BlockSpec((PEER_HEADS, eb // PEER_NKEYS, tb), lambda i, e: (0, e, i))
    key2 = pl.BlockSpec((PEER_HEADS, tiles, BF16_ROWS, tb), lambda i, e: (0, 0, 0, i), **once)
    return pl.pallas_call(
        functools.partial(_peer_kernel, eb=eb), grid=(t // tb, nblk),
        in_specs=[pl.BlockSpec((tb, d), lambda i, e: (i, 0), **once),
                  pl.BlockSpec((tb, d), lambda i, e: (i, 0), **once), key1, key1, key2, key2,
                  pl.BlockSpec((eb, d), lambda i, e: (e, 0)),
                  pl.BlockSpec((d, eb), lambda i, e: (0, e))],
        out_specs=pl.BlockSpec((tb, d), lambda i, e: (i, 0)),
        out_shape=jax.ShapeDtypeStruct((t, d), F32),
        scratch_shapes=[pltpu.VMEM((d, tb), F32),
                        pltpu.VMEM((eb // BF16_ROWS, BF16_ROWS, tb), BF16)],
        compiler_params=_cparams(("parallel", "arbitrary")), name="peer_experts",
    )(x, hn, a1, n1, b2, r2, u, vt)


def _t5_buckets(rel):
    nb = REL_BUCKETS // 2
    max_exact = nb // 2
    n = np.abs(rel)
    large = max_exact + (np.log(np.maximum(n, 1) / max_exact) / np.log(REL_MAX_DIST / max_exact)
                         * (nb - max_exact)).astype(np.int64)
    large = np.minimum(large, nb - 1)
    return np.where(rel > 0, nb, 0) + np.where(n < max_exact, n, large)


def _window_bias(rel_bias):
    span = A_BLOCK + 2 * A_WINDOW
    rel = (np.arange(span)[None, :] - A_WINDOW) - np.arange(A_BLOCK)[:, None]
    bias = jnp.transpose(rel_bias.astype(F32)[_t5_buckets(rel)], (2, 0, 1))
    bias = jnp.where(jnp.asarray(np.abs(rel) <= A_WINDOW)[None], bias, NEG)
    return bias.reshape(A_KV_HEADS, (A_HEADS // A_KV_HEADS) * A_BLOCK, span)


def _neighbourhood_patterns(rows):
    kh = min(NA_MAX_KH, rows)
    span = kh + B_QROWS
    assert rows % B_QROWS == 0 and rows >= span
    m = np.arange(rows // B_QROWS)[:, None, None]
    q_row = m * B_QROWS + np.arange(B_QROWS)[None, :, None]
    k_row = np.clip(m * B_QROWS - kh // 2, 0, rows - span) + np.arange(span)[None, None, :]
    r0 = np.clip(q_row - kh // 2, 0, rows - kh)
    row_ok = (k_row >= r0) & (k_row < r0 + kh)
    dr = np.where(row_ok, k_row - q_row + NA_MAX_KH - 1, 0)
    assert dr.min() >= 0 and dr.max() <= 2 * NA_MAX_KH - 2
    key = np.concatenate([dr.reshape(len(m), -1), row_ok.reshape(len(m), -1)], axis=1)
    _, first_idx, ids = np.unique(key, axis=0, return_index=True, return_inverse=True)
    return ids.reshape(-1).astype(np.int32), dr[first_idx], row_ok[first_idx]


def _neighbourhood_bias(rpb, rows):
    ids, dr, row_ok = _neighbourhood_patterns(rows)
    qc = np.arange(GRID_W)[:, None]
    kc = np.arange(GRID_W)[None, :]
    c_start = np.clip(qc - NA_KW // 2, 0, GRID_W - NA_KW)
    col_ok = (kc >= c_start) & (kc < c_start + NA_KW)
    dc = np.clip(kc - qc + NA_KW - 1, 0, 2 * NA_KW - 2)
    tab = rpb.astype(F32)[:, dr][..., dc]
    ok = row_ok[:, :, :, None, None] & col_ok[None, None, None]
    tab = jnp.where(jnp.asarray(ok)[None], tab, NEG)
    tab = jnp.transpose(tab, (0, 1, 2, 4, 3, 5))
    npat, nqr, nkr = dr.shape
    return tab.reshape(B_HEADS, npat, nqr * GRID_W, nkr * GRID_W), jnp.asarray(ids)


def _rope_tables(seq):
    half = HEAD_DIM // 2
    nf = half // 2
    inv = (ROPE_THETA ** (-np.arange(nf) * 2.0 / half)).astype(np.float32).astype(np.float64)
    t = np.arange(seq)
    pos = np.stack([t // GRID_W, t % GRID_W], axis=1).astype(np.float64)
    ang = pos[:, :, None] * inv[None, None, :]
    cos = np.repeat(np.cos(ang), 2, axis=1).reshape(seq, 2, 2, nf)
    sin = np.repeat(np.sin(ang), 2, axis=1).reshape(seq, 2, 2, nf)
    sin = sin * np.array([-1.0, 1.0])[None, None, :, None]
    cos = np.tile(cos.reshape(seq, HEAD_DIM), (1, CHUNK // HEAD_DIM))
    sin = np.tile(sin.reshape(seq, HEAD_DIM), (1, CHUNK // HEAD_DIM))
    return jnp.asarray(cos, F32), jnp.asarray(sin, F32)


def _permute_cols(w):
    return jnp.concatenate([w[..., _REF_SECTIONS[n][0]:_REF_SECTIONS[n][0] + _REF_SECTIONS[n][1]]
                            for n in _PERM_ORDER], axis=-1)


def _proj_gains(ga, gb, gc):
    scale = HEAD_DIM ** -0.5
    one = jnp.ones((HEAD_DIM,), F32)
    per = dict(qa=(ga[0] * scale, A_HEADS), ka=(ga[1], A_KV_HEADS), va=(one, A_KV_HEADS),
               qb=(gb[0] * scale, B_HEADS), kb=(gb[1], B_HEADS), vb=(one, B_HEADS),
               qc=(gc[0] * scale, C_HEADS), kc=(gc[1], C_KV_HEADS), vc=(one, C_KV_HEADS))
    return jnp.concatenate([jnp.tile(per[n][0].astype(F32), per[n][1]) for n in _PERM_ORDER])[None]


def kernel(x, mem, t5_rel_bias, norm_mix, w_in, qk_norm_a, sink_a, qk_norm_b, rpb_b, qk_norm_c,
           out_norm, w_out, norm_mem, norm_mem_kv, w_mem_q, w_mem_kv, qk_norm_mem, w_mem_o,
           norm_ffn, peer_w_q, peer_keys, peer_u, peer_v):
    b, s, d = x.shape
    depth = w_in.shape[0]
    t = b * s
    cos, sin = _rope_tables(s)
    blockdiag = jnp.asarray(np.kron(np.eye(CHUNK // HEAD_DIM), np.ones((HEAD_DIM, HEAD_DIM))), BF16)
    bias_a = _window_bias(t5_rel_bias)
    xf = x.reshape(t, d)
    for l in range(depth):
        qkv = _inproj(xf, norm_mix[l][None], _permute_cols(w_in[l]).astype(BF16),
                      _proj_gains(qk_norm_a[l], qk_norm_b[l], qk_norm_c[l]), cos, sin,
                      blockdiag, s).reshape(b, s, D_IN)
        g_out = out_norm[l]
        ma = _attn_a(qkv, sink_a[l].astype(F32), bias_a, g_out[None, :A_Q])
        bias_b, ids_b = _neighbourhood_bias(rpb_b[l], s // GRID_W)
        mb = _attn_b(qkv, bias_b, ids_b, g_out[None, A_Q:A_Q + B_W])
        mc = _attn_c(qkv, g_out[None, A_Q + B_W:])
        xf = _outproj(xf, ma.reshape(t, A_Q), mb.reshape(t, B_W), mc.reshape(t, C_Q),
                      w_out[l].astype(BF16))
        km, vm = _memkv(mem, norm_mem_kv[l][None], w_mem_kv[l].astype(BF16), qk_norm_mem[l, 1][None])
        xf = _memattn(xf.reshape(b, s, d), norm_mem[l][None], w_mem_q[l].astype(BF16),
                      qk_norm_mem[l, 0][None], km, vm, w_mem_o[l].astype(BF16)).reshape(t, d)
        keys = peer_keys[l].reshape(2 * PEER_HEADS, PEER_NKEYS, PEER_DKEY // 2).astype(BF16)
        hn, a1, n1, b2, r2 = _router(xf, norm_ffn[l][None], peer_w_q[l].astype(BF16), keys)
        fold = (PEER_HEADS, PEER_NKEYS // BF16_ROWS, BF16_ROWS, t)
        xf = _peer(xf, hn, a1, n1, b2.reshape(fold), r2.reshape(fold),
                   peer_u[l].astype(BF16), peer_v[l].astype(BF16).T)
    return xf.reshape(b, s, d)
```

```python
import functools

import numpy as np
import jax
import jax.numpy as jnp
from jax import lax
from jax.experimental import pallas as pl
from jax.experimental.pallas import tpu as pltpu

F32 = jnp.float32
BF16 = jnp.bfloat16

GRID_W = 64
HEAD_DIM = 64
EPS = 1e-6
NEG = -1e30
A_HEADS, A_KV_HEADS, A_WINDOW, A_BLOCK = 12, 4, 128, 128
B_HEADS, NA_MAX_KH, NA_KW = 8, 8, 16
B_QROWS = 4
C_HEADS, C_KV_HEADS, C_BLOCK = 12, 4, 128
ROPE_THETA = 10000.0
REL_BUCKETS, REL_MAX_DIST = 32, 128
MEM_HEADS, MEM_HEAD_DIM = 4, 128
MEM_W = MEM_HEADS * MEM_HEAD_DIM
PEER_HEADS, PEER_NKEYS, PEER_DKEY, PEER_TOPK = 8, 128, 256, 16
A_Q, A_KV = A_HEADS * HEAD_DIM, A_KV_HEADS * HEAD_DIM
B_W = B_HEADS * HEAD_DIM
C_Q, C_KV = C_HEADS * HEAD_DIM, C_KV_HEADS * HEAD_DIM
D_MIX = A_Q + B_W + C_Q
D_IN = A_Q + 2 * A_KV + 3 * B_W + C_Q + 2 * C_KV

LANE = 128
BF16_ROWS = 16
PEER_EB = 1024
CHUNK = 256
VMEM_LIMIT = 56 * 1024 * 1024

_REF_SECTIONS = dict(qa=(0, A_Q), ka=(A_Q, A_KV), va=(A_Q + A_KV, A_KV),
                     qb=(A_Q + 2 * A_KV, B_W), kb=(A_Q + 2 * A_KV + B_W, B_W),
                     vb=(A_Q + 2 * A_KV + 2 * B_W, B_W),
                     qc=(A_Q + 2 * A_KV + 3 * B_W, C_Q),
                     kc=(A_Q + 2 * A_KV + 3 * B_W + C_Q, C_KV),
                     vc=(A_Q + 2 * A_KV + 3 * B_W + C_Q + C_KV, C_KV))
_PERM_ORDER = ("qa", "qc", "qb", "kb", "vb", "ka", "va", "kc", "vc")
_KIND = dict(qa="norm", qc="rope", qb="norm", kb="norm", vb="plain", ka="norm", va="plain",
             kc="rope", vc="plain")
_PERM_OFF = {}
_off = 0
for _n in _PERM_ORDER:
    _PERM_OFF[_n] = _off
    _off += _REF_SECTIONS[_n][1]
_CHUNK_KINDS = []
for _n in _PERM_ORDER:
    _CHUNK_KINDS += [_KIND[_n]] * (_REF_SECTIONS[_n][1] // CHUNK)


def _cparams(sem):
    return pltpu.CompilerParams(dimension_semantics=sem, vmem_limit_bytes=VMEM_LIMIT)


def _resident(shape):
    nd = len(shape)
    return pl.BlockSpec(shape, lambda *_: (0,) * nd, pipeline_mode=pl.Buffered(1))


def _rms(x, g):
    return x * lax.rsqrt(jnp.mean(x * x, axis=-1, keepdims=True) + EPS) * g


def _pv(p, v):
    half = (p.shape[1] // 2) // LANE * LANE
    return (jnp.dot(p[:, :half], v[:half], preferred_element_type=F32)
            + jnp.dot(p[:, half:], v[half:], preferred_element_type=F32))


def _dot_nt(a, b):
    return lax.dot_general(a, b, (((1,), (1,)), ((), ())), preferred_element_type=F32)


def _inproj_kernel(x_ref, g_ref, w_ref, gain_ref, cos_ref, sin_ref, bd_ref, qkv_ref):
    hn = _rms(x_ref[...], g_ref[...]).astype(BF16)
    lane = lax.broadcasted_iota(jnp.int32, (1, CHUNK), 1)
    lower_half = (lane % 32) < 16
    for c, kind in enumerate(_CHUNK_KINDS):
        cols = slice(c * CHUNK, (c + 1) * CHUNK)
        if c % 2 == 0:
            pair = jnp.dot(hn, w_ref[:, c * CHUNK:(c + 2) * CHUNK], preferred_element_type=F32)
        acc = pair[:, (c % 2) * CHUNK:(c % 2 + 1) * CHUNK]
        if kind != "plain":
            ss = jnp.dot((acc * acc).astype(BF16), bd_ref[...], preferred_element_type=F32)
            acc = acc * lax.rsqrt(ss * (1.0 / HEAD_DIM) + EPS) * gain_ref[:, cols]
        if kind == "rope":
            partner = jnp.where(lower_half, pltpu.roll(acc, CHUNK - 16, 1), pltpu.roll(acc, 16, 1))
            acc = acc * cos_ref[...] + partner * sin_ref[...]
        qkv_ref[:, cols] = acc.astype(BF16)


def _inproj(x, g, w, gain, cos, sin, bd, seq, tm=512):
    t, d = x.shape
    n = w.shape[1]
    nseq = seq // tm
    tab = pl.BlockSpec((tm, CHUNK), lambda i: (i % nseq, 0))
    return pl.pallas_call(
        _inproj_kernel, grid=(t // tm,),
        in_specs=[pl.BlockSpec((tm, d), lambda i: (i, 0)), _resident((1, d)), _resident((d, n)),
                  _resident((1, n)), tab, tab, _resident((CHUNK, CHUNK))],
        out_specs=pl.BlockSpec((tm, n), lambda i: (i, 0)),
        out_shape=jax.ShapeDtypeStruct((t, n), BF16), compiler_params=_cparams(("parallel",)),
        name="inproj")(x, g, w, gain, cos, sin, bd)


def _attn_a_kernel(sink_ref, q_ref, kp_ref, kc_ref, kn_ref, vp_ref, vc_ref, vn_ref, bias_ref,
                   gout_ref, o_ref):
    j = pl.program_id(1)
    nb = pl.num_programs(1)
    q = q_ref[0]
    k = jnp.concatenate([kp_ref[0], kc_ref[0], kn_ref[0]], axis=0)
    v = jnp.concatenate([vp_ref[0], vc_ref[0], vn_ref[0]], axis=0)
    col = lax.broadcasted_iota(jnp.int32, (1, 3 * A_BLOCK), 1)
    first_ok = jnp.where(j > 0, 0, A_BLOCK)
    end_ok = jnp.where(j < nb - 1, 3 * A_BLOCK, 2 * A_BLOCK)
    ok = (col >= first_ok) & (col < end_ok)
    g = A_HEADS // A_KV_HEADS
    row = lax.broadcasted_iota(jnp.int32, (g * A_BLOCK, 1), 0)
    outs = []
    for hk in range(A_KV_HEADS):
        kh = k[:, hk * HEAD_DIM:(hk + 1) * HEAD_DIM]
        vh = v[:, hk * HEAD_DIM:(hk + 1) * HEAD_DIM]
        qs = jnp.concatenate([q[:, (hk * g + gi) * HEAD_DIM:(hk * g + gi + 1) * HEAD_DIM]
                              for gi in range(g)], axis=0)
        s = jnp.full((g * A_BLOCK, 1), sink_ref[hk * g], F32)
        for gi in range(1, g):
            s = jnp.where(row >= gi * A_BLOCK, sink_ref[hk * g + gi], s)
        logits = jnp.where(ok, _dot_nt(qs, kh) + bias_ref[hk], NEG)
        m = jnp.maximum(jnp.max(logits, axis=-1, keepdims=True), s)
        p = jnp.exp(logits - m)
        denom = jnp.sum(p, axis=-1, keepdims=True) + jnp.exp(s - m)
        o = _pv(p.astype(BF16), vh) / denom
        outs += [o[gi * A_BLOCK:(gi + 1) * A_BLOCK] for gi in range(g)]
    o = jnp.concatenate(outs, axis=-1)
    o_ref[0] = _rms(o, gout_ref[...]).astype(o_ref.dtype)


def _attn_a(qkv, sink, bias, gout):
    b, s, _ = qkv.shape
    nb = s // A_BLOCK
    kblk, vblk = _PERM_OFF["ka"] // A_KV, _PERM_OFF["va"] // A_KV
    qspec = pl.BlockSpec((1, A_BLOCK, A_Q), lambda bi, j: (bi, j, _PERM_OFF["qa"] // A_Q))

    def kv(blk, shift):
        return pl.BlockSpec((1, A_BLOCK, A_KV),
                            lambda bi, j: (bi, jnp.clip(j + shift, 0, nb - 1), blk))
    return pl.pallas_call(
        _attn_a_kernel, grid=(b, nb),
        in_specs=[pl.BlockSpec(memory_space=pltpu.SMEM), qspec,
                  kv(kblk, -1), kv(kblk, 0), kv(kblk, 1), kv(vblk, -1), kv(vblk, 0), kv(vblk, 1),
                  _resident(bias.shape), _resident((1, A_Q))],
        out_specs=pl.BlockSpec((1, A_BLOCK, A_Q), lambda bi, j: (bi, j, 0)),
        out_shape=jax.ShapeDtypeStruct((b, s, A_Q), BF16),
        compiler_params=_cparams(("parallel", "parallel")), name="attn_a",
    )(sink, qkv, qkv, qkv, qkv, qkv, qkv, qkv, bias, gout)


def _attn_b_kernel(ids_ref, q_ref, k_ref, v_ref, bias_ref, gout_ref, o_ref, *, rows, kh):
    del ids_ref
    m = pl.program_id(1)
    span = kh + B_QROWS
    first = jnp.clip(m * B_QROWS - kh // 2, 0, rows - span)
    start = pl.multiple_of(first * GRID_W, GRID_W)
    q = q_ref[0]
    k = k_ref[0, pl.ds(start, span * GRID_W), :]
    v = v_ref[0, pl.ds(start, span * GRID_W), :]
    outs = []
    for h in range(B_HEADS):
        hs = slice(h * HEAD_DIM, (h + 1) * HEAD_DIM)
        logits = _dot_nt(q[:, hs], k[:, hs]) + bias_ref[h, 0]
        m = jnp.max(logits, axis=-1, keepdims=True)
        p = jnp.exp(logits - m)
        denom = jnp.sum(p, axis=-1, keepdims=True)
        outs.append(_pv(p.astype(BF16), v[:, hs]) / denom)
    o = jnp.concatenate(outs, axis=-1)
    o_ref[0] = _rms(o, gout_ref[...]).astype(o_ref.dtype)


def _attn_b(qkv, bias, bias_ids, gout):
    b, s, _ = qkv.shape
    rows = s // GRID_W
    kh = min(NA_MAX_KH, rows)
    nq, nk = B_QROWS * GRID_W, (kh + B_QROWS) * GRID_W
    grid_spec = pltpu.PrefetchScalarGridSpec(
        num_scalar_prefetch=1, grid=(b, rows // B_QROWS),
        in_specs=[pl.BlockSpec((1, nq, B_W), lambda bi, m, ids: (bi, m, _PERM_OFF["qb"] // B_W)),
                  pl.BlockSpec((1, s, B_W), lambda bi, m, ids: (bi, 0, _PERM_OFF["kb"] // B_W)),
                  pl.BlockSpec((1, s, B_W), lambda bi, m, ids: (bi, 0, _PERM_OFF["vb"] // B_W)),
                  pl.BlockSpec((B_HEADS, 1, nq, nk), lambda bi, m, ids: (0, ids[m], 0, 0)),
                  pl.BlockSpec((1, B_W), lambda bi, m, ids: (0, 0))],
        out_specs=pl.BlockSpec((1, nq, B_W), lambda bi, m, ids: (bi, m, 0)))
    return pl.pallas_call(
        functools.partial(_attn_b_kernel, rows=rows, kh=kh), grid_spec=grid_spec,
        out_shape=jax.ShapeDtypeStruct((b, s, B_W), BF16),
        compiler_params=_cparams(("parallel", "arbitrary")), name="attn_b",
    )(bias_ids, qkv, qkv, qkv, bias, gout)


def _attn_c_kernel(q_ref, k_ref, v_ref, gout_ref, o_ref):
    q = q_ref[0]
    k = k_ref[0]
    v = v_ref[0]
    g = C_HEADS // C_KV_HEADS
    outs = []
    for hk in range(C_KV_HEADS):
        kh = k[:, hk * HEAD_DIM:(hk + 1) * HEAD_DIM]
        vh = v[:, hk * HEAD_DIM:(hk + 1) * HEAD_DIM]
        qs = jnp.concatenate([q[:, (hk * g + gi) * HEAD_DIM:(hk * g + gi + 1) * HEAD_DIM]
                              for gi in range(g)], axis=0)
        logits = _dot_nt(qs, kh)
        m = jnp.max(logits, axis=-1, keepdims=True)
        p = jnp.exp(logits - m)
        denom = jnp.sum(p, axis=-1, keepdims=True)
        o = _pv(p.astype(BF16), vh) / denom
        outs += [o[gi * C_BLOCK:(gi + 1) * C_BLOCK] for gi in range(g)]
    o = jnp.concatenate(outs, axis=-1)
    o_ref[0] = _rms(o, gout_ref[...]).astype(o_ref.dtype)


def _attn_c(qkv, gout):
    b, s, _ = qkv.shape
    return pl.pallas_call(
        _attn_c_kernel, grid=(b, s // C_BLOCK),
        in_specs=[pl.BlockSpec((1, C_BLOCK, C_Q), lambda bi, j: (bi, j, _PERM_OFF["qc"] // C_Q)),
                  pl.BlockSpec((1, s, C_KV), lambda bi, j: (bi, 0, _PERM_OFF["kc"] // C_KV)),
                  pl.BlockSpec((1, s, C_KV), lambda bi, j: (bi, 0, _PERM_OFF["vc"] // C_KV)),
                  _resident((1, C_Q))],
        out_specs=pl.BlockSpec((1, C_BLOCK, C_Q), lambda bi, j: (bi, j, 0)),
        out_shape=jax.ShapeDtypeStruct((b, s, C_Q), BF16),
        compiler_params=_cparams(("parallel", "arbitrary")), name="attn_c",
    )(qkv, qkv, qkv, gout)


def _outproj_kernel(x_ref, a_ref, b_ref, c_ref, w_ref, o_ref):
    acc = jnp.dot(a_ref[...], w_ref[0:A_Q, :], preferred_element_type=F32)
    acc += jnp.dot(b_ref[...], w_ref[A_Q:A_Q + B_W, :], preferred_element_type=F32)
    acc += jnp.dot(c_ref[...], w_ref[A_Q + B_W:D_MIX, :], preferred_element_type=F32)
    o_ref[...] = x_ref[...] + acc


def _outproj(x, ma, mb, mc, w, tm=512):
    t, d = x.shape

    def row(n):
        return pl.BlockSpec((tm, n), lambda i: (i, 0))
    return pl.pallas_call(
        _outproj_kernel, grid=(t // tm,),
        in_specs=[row(d), row(A_Q), row(B_W), row(C_Q), _resident(w.shape)],
        out_specs=row(d), out_shape=jax.ShapeDtypeStruct((t, d), F32),
        compiler_params=_cparams(("parallel",)), name="outproj",
    )(x, ma, mb, mc, w)


def _memkv_kernel(mem_ref, g_ref, w_ref, gk_ref, k_ref, v_ref):
    mn = _rms(mem_ref[0], g_ref[...]).astype(BF16)
    kv = jnp.dot(mn, w_ref[...], preferred_element_type=F32)
    for h in range(MEM_HEADS):
        hs = slice(h * MEM_HEAD_DIM, (h + 1) * MEM_HEAD_DIM)
        k_ref[0, :, hs] = _rms(kv[:, hs], gk_ref[...]).astype(BF16)
    v_ref[0] = kv[:, MEM_W:].astype(BF16)


def _memkv(mem, g, w, gk):
    b, m, d = mem.shape
    out = pl.BlockSpec((1, m, MEM_W), lambda i: (i, 0, 0))
    return pl.pallas_call(
        _memkv_kernel, grid=(b,),
        in_specs=[pl.BlockSpec((1, m, d), lambda i: (i, 0, 0)), _resident((1, d)),
                  _resident(w.shape), _resident((1, MEM_HEAD_DIM))],
        out_specs=[out, out], out_shape=[jax.ShapeDtypeStruct((b, m, MEM_W), BF16)] * 2,
        compiler_params=_cparams(("parallel",)), name="memkv",
    )(mem, g, w, gk)


def _memattn_kernel(x_ref, g_ref, wq_ref, gq_ref, k_ref, v_ref, wo_ref, o_ref):
    x = x_ref[0]
    hn = _rms(x, g_ref[...]).astype(BF16)
    q = jnp.dot(hn, wq_ref[...], preferred_element_type=F32)
    outs = []
    for h in range(MEM_HEADS):
        hs = slice(h * MEM_HEAD_DIM, (h + 1) * MEM_HEAD_DIM)
        qh = _rms(q[:, hs], gq_ref[...]).astype(BF16)
        logits = _dot_nt(qh, k_ref[0, :, hs]) * (MEM_HEAD_DIM ** -0.5)
        m = jnp.max(logits, axis=-1, keepdims=True)
        p = jnp.exp(logits - m)
        denom = jnp.sum(p, axis=-1, keepdims=True)
        outs.append(_pv(p.astype(BF16), v_ref[0, :, hs]) / denom)
    o = jnp.concatenate(outs, axis=-1).astype(BF16)
    o_ref[0] = x + jnp.dot(o, wo_ref[...], preferred_element_type=F32)


def _memattn(x, g, wq, gq, km, vm, wo, tm=512):
    b, s, d = x.shape
    m = km.shape[1]
    row = pl.BlockSpec((1, tm, d), lambda bi, i: (bi, i, 0))
    kv = pl.BlockSpec((1, m, MEM_W), lambda bi, i: (bi, 0, 0))
    return pl.pallas_call(
        _memattn_kernel, grid=(b, s // tm),
        in_specs=[row, _resident((1, d)), _resident(wq.shape), _resident((1, MEM_HEAD_DIM)), kv, kv,
                  _resident(wo.shape)],
        out_specs=row, out_shape=jax.ShapeDtypeStruct((b, s, d), F32),
        compiler_params=_cparams(("parallel", "arbitrary")), name="memattn",
    )(x, g, wq, gq, km, vm, wo)


def _extract_top(s, want_rank):
    cur = s
    rank = jnp.full(s.shape, float(PEER_NKEYS - 1), F32)
    tops = []
    for kk in range(PEER_TOPK):
        mk = jnp.max(cur, axis=0, keepdims=True)
        sel = cur == mk
        if want_rank:
            rank = jnp.where(sel, float(kk), rank)
        cur = jnp.where(sel, -jnp.inf, cur)
        tops.append(mk)
    return tops, rank


def _stack8(rows):
    sub = lax.broadcasted_iota(jnp.int32, (8, rows[0].shape[1]), 0)
    out = jnp.broadcast_to(rows[0], sub.shape)
    for kk in range(1, 8):
        out = jnp.where(sub == kk, rows[kk], out)
    return out


def _route_head(s1, s2):
    t1, _ = _extract_top(s1, False)
    t2, r2 = _extract_top(s2, True)
    ts2_lo = _stack8(t2[:8])
    ts2_hi = _stack8(t2[8:])
    sub = lax.broadcasted_iota(jnp.int32, ts2_lo.shape, 0)
    pieces = [t1[0] + ts2_lo, t1[0] + ts2_hi]
    for k1 in range(1, 8):
        pieces.append(jnp.where(sub < PEER_TOPK // (k1 + 1), t1[k1] + ts2_lo, -jnp.inf))
    pieces.append(_stack8(t1[8:]) + t2[0])
    cand = jnp.concatenate(pieces, axis=0)
    best = []
    for _ in range(PEER_TOPK):
        mk = jnp.max(cand, axis=0, keepdims=True)
        cand = jnp.where(cand == mk, -jnp.inf, cand)
        best.append(mk)
    tau = best[-1]
    top = t1[0] + t2[0]
    z = best[0] - top
    z = jnp.exp(z)
    for bk in best[1:]:
        z = z + jnp.exp(bk - top)
    n1 = jnp.zeros(s1.shape, F32)
    for k2 in range(PEER_TOPK):
        n1 = n1 + jnp.where(s1 + t2[k2] >= tau, 1.0, 0.0)
    a1 = jnp.exp(s1 - t1[0]) / z
    b2 = jnp.exp(s2 - t2[0])
    return a1, n1, b2, r2


def _router_kernel(x_ref, g_ref, wq_ref, keys_ref, hn_ref, a1_ref, n1_ref, b2_ref, r2_ref, q_scr,
                   *, tm):
    hn = _rms(x_ref[...], g_ref[...]).astype(BF16)
    hn_ref[...] = hn
    half = PEER_DKEY // 2
    wide = wq_ref.shape[1] // 2
    for part in range(2):
        q = jnp.dot(hn, wq_ref[:, part * wide:(part + 1) * wide], preferred_element_type=F32)
        for c in range(wide // half):
            q_scr[part * (wide // half) + c] = q[:, c * half:(c + 1) * half].astype(BF16)

    def head(h, carry):
        s1 = _dot_nt(keys_ref[2 * h], q_scr[2 * h])
        s2 = _dot_nt(keys_ref[2 * h + 1], q_scr[2 * h + 1])
        for lc in range(tm // LANE):
            ls = slice(lc * LANE, (lc + 1) * LANE)
            a1, n1, b2, r2 = _route_head(s1[:, ls], s2[:, ls])
            a1_ref[h, :, ls] = a1
            n1_ref[h, :, ls] = n1
            b2_ref[h, :, ls] = b2.astype(BF16)
            r2_ref[h, :, ls] = r2.astype(BF16)
        return carry
    lax.fori_loop(0, PEER_HEADS, head, 0)


def _router(x, g, wq, keys, tm=512):
    t, d = x.shape
    tab = pl.BlockSpec((PEER_HEADS, PEER_NKEYS, tm), lambda i: (0, 0, i))
    tab_shape = (PEER_HEADS, PEER_NKEYS, t)
    return pl.pallas_call(
        functools.partial(_router_kernel, tm=tm), grid=(t // tm,),
        in_specs=[pl.BlockSpec((tm, d), lambda i: (i, 0)), _resident((1, d)), _resident(wq.shape),
                  _resident(keys.shape)],
        out_specs=[pl.BlockSpec((tm, d), lambda i: (i, 0)), tab, tab, tab, tab],
        out_shape=[jax.ShapeDtypeStruct((t, d), BF16)]
        + [jax.ShapeDtypeStruct(tab_shape, dt) for dt in (F32, F32, BF16, BF16)],
        scratch_shapes=[pltpu.VMEM((2 * PEER_HEADS, tm, PEER_DKEY // 2), BF16)],
        compiler_params=_cparams(("parallel",)), name="peer_router",
    )(x, g, wq, keys)


def _gelu(x):
    return 0.5 * x * (1.0 + lax.erf(x * (2.0 ** -0.5)))


def _peer_kernel(x_ref, hn_ref, a1_ref, n1_ref, b2_ref, r2_ref, u_ref, vt_ref, y_ref, acc_ref, w_ref,
                 *, eb):
    e = pl.program_id(1)
    tb = hn_ref.shape[0]

    @pl.when(e == 0)
    def _():
        acc_ref[...] = jnp.zeros_like(acc_ref)

    pre = _dot_nt(u_ref[...], hn_ref[...])
    tiles = PEER_NKEYS // BF16_ROWS
    for il in range(eb // PEER_NKEYS):
        g = None
        for h in range(PEER_HEADS):
            n_i = jnp.broadcast_to(n1_ref[h, il:il + 1, :], (BF16_ROWS, tb)).astype(BF16)[None]
            a_i = jnp.broadcast_to(a1_ref[h, il:il + 1, :], (BF16_ROWS, tb)).astype(BF16)[None]
            term = jnp.where(r2_ref[h] < n_i, b2_ref[h], 0.0) * a_i
            g = term if g is None else g + term
        act = _gelu(pre[il * PEER_NKEYS:(il + 1) * PEER_NKEYS, :]).astype(BF16)
        w_ref[il * tiles:(il + 1) * tiles] = act.reshape(tiles, BF16_ROWS, tb) * g
    acc_ref[...] += jnp.dot(vt_ref[0], w_ref[...].reshape(eb, tb), preferred_element_type=F32)

    @pl.when(e == pl.num_programs(1) - 1)
    def _():
        y_ref[...] = x_ref[...] + acc_ref[...].T


def _peer(x, hn, a1, n1, b2, r2, u, vt, tb=512):
    t, d = hn.shape
    nblk, _, eb = vt.shape
    tiles = PEER_NKEYS // BF16_ROWS
    once = dict(pipeline_mode=pl.Buffered(1))
    key1 = pl.BlockSpec((PEER_HEADS, eb // PEER_NKEYS, tb), lambda i, e: (0, e, i))
    key2 = pl.BlockSpec((PEER_HEADS, tiles, BF16_ROWS, tb), lambda i, e: (0, 0, 0, i), **once)
    return pl.pallas_call(
        functools.partial(_peer_kernel, eb=eb), grid=(t // tb, nblk),
        in_specs=[pl.BlockSpec((tb, d), lambda i, e: (i, 0), **once),
                  pl.BlockSpec((tb, d), lambda i, e: (i, 0), **once), key1, key1, key2, key2,
                  pl.BlockSpec((eb, d), lambda i, e: (e, 0)),
                  pl.BlockSpec((1, d, eb), lambda i, e: (e, 0, 0))],
        out_specs=pl.BlockSpec((tb, d), lambda i, e: (i, 0)),
        out_shape=jax.ShapeDtypeStruct((t, d), F32),
        scratch_shapes=[pltpu.VMEM((d, tb), F32),
                        pltpu.VMEM((eb // BF16_ROWS, BF16_ROWS, tb), BF16)],
        compiler_params=_cparams(("parallel", "arbitrary")), name="peer_experts",
    )(x, hn, a1, n1, b2, r2, u, vt)


def _t5_buckets(rel):
    nb = REL_BUCKETS // 2
    max_exact = nb // 2
    n = np.abs(rel)
    large = max_exact + (np.log(np.maximum(n, 1) / max_exact) / np.log(REL_MAX_DIST / max_exact)
                         * (nb - max_exact)).astype(np.int64)
    large = np.minimum(large, nb - 1)
    return np.where(rel > 0, nb, 0) + np.where(n < max_exact, n, large)


def _window_bias(rel_bias):
    span = A_BLOCK + 2 * A_WINDOW
    rel = (np.arange(span)[None, :] - A_WINDOW) - np.arange(A_BLOCK)[:, None]
    bias = jnp.transpose(rel_bias.astype(F32)[_t5_buckets(rel)], (2, 0, 1))
    bias = jnp.where(jnp.asarray(np.abs(rel) <= A_WINDOW)[None], bias, NEG)
    return bias.reshape(A_KV_HEADS, (A_HEADS // A_KV_HEADS) * A_BLOCK, span)


def _neighbourhood_patterns(rows):
    kh = min(NA_MAX_KH, rows)
    span = kh + B_QROWS
    assert rows % B_QROWS == 0 and rows >= span
    m = np.arange(rows // B_QROWS)[:, None, None]
    q_row = m * B_QROWS + np.arange(B_QROWS)[None, :, None]
    k_row = np.clip(m * B_QROWS - kh // 2, 0, rows - span) + np.arange(span)[None, None, :]
    r0 = np.clip(q_row - kh // 2, 0, rows - kh)
    row_ok = (k_row >= r0) & (k_row < r0 + kh)
    dr = np.where(row_ok, k_row - q_row + NA_MAX_KH - 1, 0)
    assert dr.min() >= 0 and dr.max() <= 2 * NA_MAX_KH - 2
    key = np.concatenate([dr.reshape(len(m), -1), row_ok.reshape(len(m), -1)], axis=1)
    _, first_idx, ids = np.unique(key, axis=0, return_index=True, return_inverse=True)
    return ids.reshape(-1).astype(np.int32), dr[first_idx], row_ok[first_idx]


def _neighbourhood_bias(rpb, rows):
    ids, dr, row_ok = _neighbourhood_patterns(rows)
    qc = np.arange(GRID_W)[:, None]
    kc = np.arange(GRID_W)[None, :]
    c_start = np.clip(qc - NA_KW // 2, 0, GRID_W - NA_KW)
    col_ok = (kc >= c_start) & (kc < c_start + NA_KW)
    dc = np.clip(kc - qc + NA_KW - 1, 0, 2 * NA_KW - 2)
    tab = rpb.astype(F32)[:, dr][..., dc]
    ok = row_ok[:, :, :, None, None] & col_ok[None, None, None]
    tab = jnp.where(jnp.asarray(ok)[None], tab, NEG)
    tab = jnp.transpose(tab, (0, 1, 2, 4, 3, 5))
    npat, nqr, nkr = dr.shape
    return tab.reshape(B_HEADS, npat, nqr * GRID_W, nkr * GRID_W), jnp.asarray(ids)


def _rope_tables(seq):
    half = HEAD_DIM // 2
    nf = half // 2
    inv = (ROPE_THETA ** (-np.arange(nf) * 2.0 / half)).astype(np.float32).astype(np.float64)
    t = np.arange(seq)
    pos = np.stack([t // GRID_W, t % GRID_W], axis=1).astype(np.float64)
    ang = pos[:, :, None] * inv[None, None, :]
    cos = np.repeat(np.cos(ang), 2, axis=1).reshape(seq, 2, 2, nf)
    sin = np.repeat(np.sin(ang), 2, axis=1).reshape(seq, 2, 2, nf)
    sin = sin * np.array([-1.0, 1.0])[None, None, :, None]
    cos = np.tile(cos.reshape(seq, HEAD_DIM), (1, CHUNK // HEAD_DIM))
    sin = np.tile(sin.reshape(seq, HEAD_DIM), (1, CHUNK // HEAD_DIM))
    return jnp.asarray(cos, F32), jnp.asarray(sin, F32)


def _permute_cols(w):
    return jnp.concatenate([w[..., _REF_SECTIONS[n][0]:_REF_SECTIONS[n][0] + _REF_SECTIONS[n][1]]
                            for n in _PERM_ORDER], axis=-1)


def _proj_gains(ga, gb, gc):
    scale = HEAD_DIM ** -0.5
    one = jnp.ones((HEAD_DIM,), F32)
    per = dict(qa=(ga[0] * scale, A_HEADS), ka=(ga[1], A_KV_HEADS), va=(one, A_KV_HEADS),
               qb=(gb[0] * scale, B_HEADS), kb=(gb[1], B_HEADS), vb=(one, B_HEADS),
               qc=(gc[0] * scale, C_HEADS), kc=(gc[1], C_KV_HEADS), vc=(one, C_KV_HEADS))
    return jnp.concatenate([jnp.tile(per[n][0].astype(F32), per[n][1]) for n in _PERM_ORDER])[None]


def kernel(x, mem, t5_rel_bias, norm_mix, w_in, qk_norm_a, sink_a, qk_norm_b, rpb_b, qk_norm_c,
           out_norm, w_out, norm_mem, norm_mem_kv, w_mem_q, w_mem_kv, qk_norm_mem, w_mem_o,
           norm_ffn, peer_w_q, peer_keys, peer_u, peer_v):
    b, s, d = x.shape
    depth = w_in.shape[0]
    t = b * s
    cos, sin = _rope_tables(s)
    blockdiag = jnp.asarray(np.kron(np.eye(CHUNK // HEAD_DIM), np.ones((HEAD_DIM, HEAD_DIM))), BF16)
    bias_a = _window_bias(t5_rel_bias)
    xf = x.reshape(t, d)
    for l in range(depth):
        qkv = _inproj(xf, norm_mix[l][None], _permute_cols(w_in[l]).astype(BF16),
                      _proj_gains(qk_norm_a[l], qk_norm_b[l], qk_norm_c[l]), cos, sin,
                      blockdiag, s).reshape(b, s, D_IN)
        g_out = out_norm[l]
        ma = _attn_a(qkv, sink_a[l].astype(F32), bias_a, g_out[None, :A_Q])
        bias_b, ids_b = _neighbourhood_bias(rpb_b[l], s // GRID_W)
        mb = _attn_b(qkv, bias_b, ids_b, g_out[None, A_Q:A_Q + B_W])
        mc = _attn_c(qkv, g_out[None, A_Q + B_W:])
        xf = _outproj(xf, ma.reshape(t, A_Q), mb.reshape(t, B_W), mc.reshape(t, C_Q),
                      w_out[l].astype(BF16))
        km, vm = _memkv(mem, norm_mem_kv[l][None], w_mem_kv[l].astype(BF16), qk_norm_mem[l, 1][None])
        xf = _memattn(xf.reshape(b, s, d), norm_mem[l][None], w_mem_q[l].astype(BF16),
                      qk_norm_mem[l, 0][None], km, vm, w_mem_o[l].astype(BF16)).reshape(t, d)
        keys = peer_keys[l].reshape(2 * PEER_HEADS, PEER_NKEYS, PEER_DKEY // 2).astype(BF16)
        hn, a1, n1, b2, r2 = _router(xf, norm_ffn[l][None], peer_w_q[l].astype(BF16), keys)
        fold = (PEER_HEADS, PEER_NKEYS // BF16_ROWS, BF16_ROWS, t)
        xf = _peer(xf, hn, a1, n1, b2.reshape(fold), r2.reshape(fold),
                   peer_u[l].astype(BF16),
                   peer_v[l].astype(BF16).reshape(-1, PEER_EB, d).transpose(0, 2, 1))
    return xf.reshape(b, s, d)
```

```python
import functools

import numpy as np
import jax
import jax.numpy as jnp
from jax import lax
from jax.experimental import pallas as pl
from jax.experimental.pallas import tpu as pltpu

F32 = jnp.float32
BF16 = jnp.bfloat16

GRID_W = 64
HEAD_DIM = 64
EPS = 1e-6
NEG = -1e30
A_HEADS, A_KV_HEADS, A_WINDOW, A_BLOCK = 12, 4, 128, 128
B_HEADS, NA_MAX_KH, NA_KW = 8, 8, 16
A_QBLOCKS = 2
B_QROWS = 4
C_HEADS, C_KV_HEADS, C_BLOCK = 12, 4, 128
ROPE_THETA = 10000.0
REL_BUCKETS, REL_MAX_DIST = 32, 128
MEM_HEADS, MEM_HEAD_DIM = 4, 128
MEM_W = MEM_HEADS * MEM_HEAD_DIM
PEER_HEADS, PEER_NKEYS, PEER_DKEY, PEER_TOPK = 8, 128, 256, 16
A_Q, A_KV = A_HEADS * HEAD_DIM, A_KV_HEADS * HEAD_DIM
B_W = B_HEADS * HEAD_DIM
C_Q, C_KV = C_HEADS * HEAD_DIM, C_KV_HEADS * HEAD_DIM
D_MIX = A_Q + B_W + C_Q
D_IN = A_Q + 2 * A_KV + 3 * B_W + C_Q + 2 * C_KV

LANE = 128
BF16_ROWS = 16
PEER_EB = 1024
CHUNK = 256
VMEM_LIMIT = 56 * 1024 * 1024

_REF_SECTIONS = dict(qa=(0, A_Q), ka=(A_Q, A_KV), va=(A_Q + A_KV, A_KV),
                     qb=(A_Q + 2 * A_KV, B_W), kb=(A_Q + 2 * A_KV + B_W, B_W),
                     vb=(A_Q + 2 * A_KV + 2 * B_W, B_W),
                     qc=(A_Q + 2 * A_KV + 3 * B_W, C_Q),
                     kc=(A_Q + 2 * A_KV + 3 * B_W + C_Q, C_KV),
                     vc=(A_Q + 2 * A_KV + 3 * B_W + C_Q + C_KV, C_KV))
_PERM_ORDER = ("qa", "qc", "qb", "kb", "vb", "ka", "va", "kc", "vc")
_KIND = dict(qa="norm", qc="rope", qb="norm", kb="norm", vb="plain", ka="norm", va="plain",
             kc="rope", vc="plain")
_PERM_OFF = {}
_off = 0
for _n in _PERM_ORDER:
    _PERM_OFF[_n] = _off
    _off += _REF_SECTIONS[_n][1]
_CHUNK_KINDS = []
for _n in _PERM_ORDER:
    _CHUNK_KINDS += [_KIND[_n]] * (_REF_SECTIONS[_n][1] // CHUNK)


def _cparams(sem):
    return pltpu.CompilerParams(dimension_semantics=sem, vmem_limit_bytes=VMEM_LIMIT)


def _resident(shape):
    nd = len(shape)
    return pl.BlockSpec(shape, lambda *_: (0,) * nd, pipeline_mode=pl.Buffered(1))


def _rms(x, g):
    return x * lax.rsqrt(jnp.mean(x * x, axis=-1, keepdims=True) + EPS) * g


def _pv(p, v):
    half = (p.shape[1] // 2) // LANE * LANE
    return (jnp.dot(p[:, :half], v[:half], preferred_element_type=F32)
            + jnp.dot(p[:, half:], v[half:], preferred_element_type=F32))


def _dot_nt(a, b):
    return lax.dot_general(a, b, (((1,), (1,)), ((), ())), preferred_element_type=F32)


def _inproj_kernel(x_ref, g_ref, w_ref, gain_ref, cos_ref, sin_ref, bd_ref, qkv_ref):
    hn = _rms(x_ref[...], g_ref[...]).astype(BF16)
    lane = lax.broadcasted_iota(jnp.int32, (1, CHUNK), 1)
    lower_half = (lane % 32) < 16
    for c, kind in enumerate(_CHUNK_KINDS):
        cols = slice(c * CHUNK, (c + 1) * CHUNK)
        if c % 2 == 0:
            pair = jnp.dot(hn, w_ref[:, c * CHUNK:(c + 2) * CHUNK], preferred_element_type=F32)
        acc = pair[:, (c % 2) * CHUNK:(c % 2 + 1) * CHUNK]
        if kind != "plain":
            ss = jnp.dot((acc * acc).astype(BF16), bd_ref[...], preferred_element_type=F32)
            acc = acc * lax.rsqrt(ss * (1.0 / HEAD_DIM) + EPS) * gain_ref[:, cols]
        if kind == "rope":
            partner = jnp.where(lower_half, pltpu.roll(acc, CHUNK - 16, 1), pltpu.roll(acc, 16, 1))
            acc = acc * cos_ref[...] + partner * sin_ref[...]
        qkv_ref[:, cols] = acc.astype(BF16)


def _inproj(x, g, w, gain, cos, sin, bd, seq, tm=512):
    t, d = x.shape
    n = w.shape[1]
    nseq = seq // tm
    tab = pl.BlockSpec((tm, CHUNK), lambda i: (i % nseq, 0))
    return pl.pallas_call(
        _inproj_kernel, grid=(t // tm,),
        in_specs=[pl.BlockSpec((tm, d), lambda i: (i, 0)), _resident((1, d)), _resident((d, n)),
                  _resident((1, n)), tab, tab, _resident((CHUNK, CHUNK))],
        out_specs=pl.BlockSpec((tm, n), lambda i: (i, 0)),
        out_shape=jax.ShapeDtypeStruct((t, n), BF16), compiler_params=_cparams(("parallel",)),
        name="inproj")(x, g, w, gain, cos, sin, bd)


def _attn_a_kernel(sink_ref, q_ref, k_ref, v_ref, bias_ref, gout_ref, o_ref):
    nb = k_ref.shape[1] // A_BLOCK
    for sub in range(A_QBLOCKS):
        j = pl.program_id(1) * A_QBLOCKS + sub
        rows = slice(sub * A_BLOCK, (sub + 1) * A_BLOCK)
        o_ref[0, rows, :] = _window_block(sink_ref, q_ref[0, rows, :], k_ref, v_ref, bias_ref,
                                          gout_ref, j, nb).astype(o_ref.dtype)


def _window_block(sink_ref, q, k_ref, v_ref, bias_ref, gout_ref, j, nb):
    def three(ref):
        return jnp.concatenate(
            [ref[0, pl.ds(pl.multiple_of(jnp.clip(j + d, 0, nb - 1) * A_BLOCK, A_BLOCK), A_BLOCK), :]
             for d in (-1, 0, 1)], axis=0)
    k = three(k_ref)
    v = three(v_ref)
    col = lax.broadcasted_iota(jnp.int32, (1, 3 * A_BLOCK), 1)
    first_ok = jnp.where(j > 0, 0, A_BLOCK)
    end_ok = jnp.where(j < nb - 1, 3 * A_BLOCK, 2 * A_BLOCK)
    ok = (col >= first_ok) & (col < end_ok)
    g = A_HEADS // A_KV_HEADS
    row = lax.broadcasted_iota(jnp.int32, (g * A_BLOCK, 1), 0)
    outs = []
    for hk in range(A_KV_HEADS):
        kh = k[:, hk * HEAD_DIM:(hk + 1) * HEAD_DIM]
        vh = v[:, hk * HEAD_DIM:(hk + 1) * HEAD_DIM]
        qs = jnp.concatenate([q[:, (hk * g + gi) * HEAD_DIM:(hk * g + gi + 1) * HEAD_DIM]
                              for gi in range(g)], axis=0)
        s = jnp.full((g * A_BLOCK, 1), sink_ref[hk * g], F32)
        for gi in range(1, g):
            s = jnp.where(row >= gi * A_BLOCK, sink_ref[hk * g + gi], s)
        logits = jnp.where(ok, _dot_nt(qs, kh) + bias_ref[hk], NEG)
        m = jnp.maximum(jnp.max(logits, axis=-1, keepdims=True), s)
        p = jnp.exp(logits - m)
        denom = jnp.sum(p, axis=-1, keepdims=True) + jnp.exp(s - m)
        o = _pv(p.astype(BF16), vh) / denom
        outs += [o[gi * A_BLOCK:(gi + 1) * A_BLOCK] for gi in range(g)]
    return _rms(jnp.concatenate(outs, axis=-1), gout_ref[...])


def _attn_a(qkv, sink, bias, gout):
    b, s, _ = qkv.shape
    tq = A_QBLOCKS * A_BLOCK
    return pl.pallas_call(
        _attn_a_kernel, grid=(b, s // tq),
        in_specs=[pl.BlockSpec(memory_space=pltpu.SMEM),
                  pl.BlockSpec((1, tq, A_Q), lambda bi, j: (bi, j, _PERM_OFF["qa"] // A_Q)),
                  pl.BlockSpec((1, s, A_KV), lambda bi, j: (bi, 0, _PERM_OFF["ka"] // A_KV)),
                  pl.BlockSpec((1, s, A_KV), lambda bi, j: (bi, 0, _PERM_OFF["va"] // A_KV)),
                  _resident(bias.shape), _resident((1, A_Q))],
        out_specs=pl.BlockSpec((1, tq, A_Q), lambda bi, j: (bi, j, 0)),
        out_shape=jax.ShapeDtypeStruct((b, s, A_Q), BF16),
        compiler_params=_cparams(("parallel", "arbitrary")), name="attn_a",
    )(sink, qkv, qkv, qkv, bias, gout)


def _attn_b_kernel(ids_ref, q_ref, k_ref, v_ref, bias_ref, gout_ref, o_ref, *, rows, kh):
    del ids_ref
    m = pl.program_id(1)
    span = kh + B_QROWS
    first = jnp.clip(m * B_QROWS - kh // 2, 0, rows - span)
    start = pl.multiple_of(first * GRID_W, GRID_W)
    q = q_ref[0]
    k = k_ref[0, pl.ds(start, span * GRID_W), :]
    v = v_ref[0, pl.ds(start, span * GRID_W), :]
    outs = []
    for h in range(B_HEADS):
        hs = slice(h * HEAD_DIM, (h + 1) * HEAD_DIM)
        logits = _dot_nt(q[:, hs], k[:, hs]) + bias_ref[h, 0]
        m = jnp.max(logits, axis=-1, keepdims=True)
        p = jnp.exp(logits - m)
        denom = jnp.sum(p, axis=-1, keepdims=True)
        outs.append(_pv(p.astype(BF16), v[:, hs]) / denom)
    o = jnp.concatenate(outs, axis=-1)
    o_ref[0] = _rms(o, gout_ref[...]).astype(o_ref.dtype)


def _attn_b(qkv, bias, bias_ids, gout):
    b, s, _ = qkv.shape
    rows = s // GRID_W
    kh = min(NA_MAX_KH, rows)
    nq, nk = B_QROWS * GRID_W, (kh + B_QROWS) * GRID_W
    grid_spec = pltpu.PrefetchScalarGridSpec(
        num_scalar_prefetch=1, grid=(b, rows // B_QROWS),
        in_specs=[pl.BlockSpec((1, nq, B_W), lambda bi, m, ids: (bi, m, _PERM_OFF["qb"] // B_W)),
                  pl.BlockSpec((1, s, B_W), lambda bi, m, ids: (bi, 0, _PERM_OFF["kb"] // B_W)),
                  pl.BlockSpec((1, s, B_W), lambda bi, m, ids: (bi, 0, _PERM_OFF["vb"] // B_W)),
                  pl.BlockSpec((B_HEADS, 1, nq, nk), lambda bi, m, ids: (0, ids[m], 0, 0)),
                  pl.BlockSpec((1, B_W), lambda bi, m, ids: (0, 0))],
        out_specs=pl.BlockSpec((1, nq, B_W), lambda bi, m, ids: (bi, m, 0)))
    return pl.pallas_call(
        functools.partial(_attn_b_kernel, rows=rows, kh=kh), grid_spec=grid_spec,
        out_shape=jax.ShapeDtypeStruct((b, s, B_W), BF16),
        compiler_params=_cparams(("parallel", "arbitrary")), name="attn_b",
    )(bias_ids, qkv, qkv, qkv, bias, gout)


def _attn_c_kernel(q_ref, k_ref, v_ref, gout_ref, o_ref):
    q = q_ref[0]
    k = k_ref[0]
    v = v_ref[0]
    g = C_HEADS // C_KV_HEADS
    outs = []
    for hk in range(C_KV_HEADS):
        kh = k[:, hk * HEAD_DIM:(hk + 1) * HEAD_DIM]
        vh = v[:, hk * HEAD_DIM:(hk + 1) * HEAD_DIM]
        qs = jnp.concatenate([q[:, (hk * g + gi) * HEAD_DIM:(hk * g + gi + 1) * HEAD_DIM]
                              for gi in range(g)], axis=0)
        logits = _dot_nt(qs, kh)
        m = jnp.max(logits, axis=-1, keepdims=True)
        p = jnp.exp(logits - m)
        denom = jnp.sum(p, axis=-1, keepdims=True)
        o = _pv(p.astype(BF16), vh) / denom
        outs += [o[gi * C_BLOCK:(gi + 1) * C_BLOCK] for gi in range(g)]
    o = jnp.concatenate(outs, axis=-1)
    o_ref[0] = _rms(o, gout_ref[...]).astype(o_ref.dtype)


def _attn_c(qkv, gout):
    b, s, _ = qkv.shape
    return pl.pallas_call(
        _attn_c_kernel, grid=(b, s // C_BLOCK),
        in_specs=[pl.BlockSpec((1, C_BLOCK, C_Q), lambda bi, j: (bi, j, _PERM_OFF["qc"] // C_Q)),
                  pl.BlockSpec((1, s, C_KV), lambda bi, j: (bi, 0, _PERM_OFF["kc"] // C_KV)),
                  pl.BlockSpec((1, s, C_KV), lambda bi, j: (bi, 0, _PERM_OFF["vc"] // C_KV)),
                  _resident((1, C_Q))],
        out_specs=pl.BlockSpec((1, C_BLOCK, C_Q), lambda bi, j: (bi, j, 0)),
        out_shape=jax.ShapeDtypeStruct((b, s, C_Q), BF16),
        compiler_params=_cparams(("parallel", "arbitrary")), name="attn_c",
    )(qkv, qkv, qkv, gout)


def _outproj_kernel(x_ref, a_ref, b_ref, c_ref, w_ref, o_ref):
    acc = jnp.dot(a_ref[...], w_ref[0:A_Q, :], preferred_element_type=F32)
    acc += jnp.dot(b_ref[...], w_ref[A_Q:A_Q + B_W, :], preferred_element_type=F32)
    acc += jnp.dot(c_ref[...], w_ref[A_Q + B_W:D_MIX, :], preferred_element_type=F32)
    o_ref[...] = x_ref[...] + acc


def _outproj(x, ma, mb, mc, w, tm=512):
    t, d = x.shape

    def row(n):
        return pl.BlockSpec((tm, n), lambda i: (i, 0))
    return pl.pallas_call(
        _outproj_kernel, grid=(t // tm,),
        in_specs=[row(d), row(A_Q), row(B_W), row(C_Q), _resident(w.shape)],
        out_specs=row(d), out_shape=jax.ShapeDtypeStruct((t, d), F32),
        compiler_params=_cparams(("parallel",)), name="outproj",
    )(x, ma, mb, mc, w)


def _memkv_kernel(mem_ref, g_ref, w_ref, gk_ref, k_ref, v_ref):
    mn = _rms(mem_ref[0], g_ref[...]).astype(BF16)
    kv = jnp.dot(mn, w_ref[...], preferred_element_type=F32)
    for h in range(MEM_HEADS):
        hs = slice(h * MEM_HEAD_DIM, (h + 1) * MEM_HEAD_DIM)
        k_ref[0, :, hs] = _rms(kv[:, hs], gk_ref[...]).astype(BF16)
    v_ref[0] = kv[:, MEM_W:].astype(BF16)


def _memkv(mem, g, w, gk):
    b, m, d = mem.shape
    out = pl.BlockSpec((1, m, MEM_W), lambda i: (i, 0, 0))
    return pl.pallas_call(
        _memkv_kernel, grid=(b,),
        in_specs=[pl.BlockSpec((1, m, d), lambda i: (i, 0, 0)), _resident((1, d)),
                  _resident(w.shape), _resident((1, MEM_HEAD_DIM))],
        out_specs=[out, out], out_shape=[jax.ShapeDtypeStruct((b, m, MEM_W), BF16)] * 2,
        compiler_params=_cparams(("parallel",)), name="memkv",
    )(mem, g, w, gk)


def _memattn_kernel(x_ref, g_ref, wq_ref, gq_ref, k_ref, v_ref, wo_ref, o_ref):
    x = x_ref[0]
    hn = _rms(x, g_ref[...]).astype(BF16)
    q = jnp.dot(hn, wq_ref[...], preferred_element_type=F32)
    outs = []
    for h in range(MEM_HEADS):
        hs = slice(h * MEM_HEAD_DIM, (h + 1) * MEM_HEAD_DIM)
        qh = _rms(q[:, hs], gq_ref[...]).astype(BF16)
        logits = _dot_nt(qh, k_ref[0, :, hs]) * (MEM_HEAD_DIM ** -0.5)
        m = jnp.max(logits, axis=-1, keepdims=True)
        p = jnp.exp(logits - m)
        denom = jnp.sum(p, axis=-1, keepdims=True)
        outs.append(_pv(p.astype(BF16), v_ref[0, :, hs]) / denom)
    o = jnp.concatenate(outs, axis=-1).astype(BF16)
    o_ref[0] = x + jnp.dot(o, wo_ref[...], preferred_element_type=F32)


def _memattn(x, g, wq, gq, km, vm, wo, tm=512):
    b, s, d = x.shape
    m = km.shape[1]
    row = pl.BlockSpec((1, tm, d), lambda bi, i: (bi, i, 0))
    kv = pl.BlockSpec((1, m, MEM_W), lambda bi, i: (bi, 0, 0))
    return pl.pallas_call(
        _memattn_kernel, grid=(b, s // tm),
        in_specs=[row, _resident((1, d)), _resident(wq.shape), _resident((1, MEM_HEAD_DIM)), kv, kv,
                  _resident(wo.shape)],
        out_specs=row, out_shape=jax.ShapeDtypeStruct((b, s, d), F32),
        compiler_params=_cparams(("parallel", "arbitrary")), name="memattn",
    )(x, g, wq, gq, km, vm, wo)


def _sorting_network(n):
    def merge(lo, hi, r):
        step = r * 2
        if step < hi - lo:
            yield from merge(lo, hi, step)
            yield from merge(lo + r, hi, step)
            yield from ((i, i + r) for i in range(lo + r, hi - r, step))
        else:
            yield (lo, lo + r)

    def sort(lo, hi):
        if hi - lo >= 1:
            mid = lo + (hi - lo) // 2
            yield from sort(lo, mid)
            yield from sort(mid + 1, hi)
            yield from merge(lo, hi, 1)
    return tuple(sort(0, n - 1))


_SORT16 = _sorting_network(PEER_NKEYS // 8)


def _top_values(s):
    depth = PEER_NKEYS // 8
    v = [s[8 * i:8 * (i + 1)] for i in range(depth)]
    for i, j in _SORT16:
        v[i], v[j] = jnp.maximum(v[i], v[j]), jnp.minimum(v[i], v[j])
    tops = []
    for kk in range(PEER_TOPK):
        mk = jnp.max(v[0], axis=0, keepdims=True)
        tops.append(mk)
        pop = v[0] == mk
        for i in range(PEER_TOPK - 1 - kk):
            v[i] = jnp.where(pop, v[i + 1], v[i])
    return tops


def _rank_among(tops, s):
    rank = jnp.full(s.shape, float(PEER_TOPK), F32)
    for kk in reversed(range(PEER_TOPK)):
        rank = jnp.where(s >= tops[kk], float(kk), rank)
    return rank


def _stack8(rows):
    sub = lax.broadcasted_iota(jnp.int32, (8, rows[0].shape[1]), 0)
    out = jnp.broadcast_to(rows[0], sub.shape)
    for kk in range(1, 8):
        out = jnp.where(sub == kk, rows[kk], out)
    return out


def _route_head(s1, s2):
    t1 = _top_values(s1)
    t2 = _top_values(s2)
    r2 = _rank_among(t2, s2)
    ts2_lo = _stack8(t2[:8])
    ts2_hi = _stack8(t2[8:])
    sub = lax.broadcasted_iota(jnp.int32, ts2_lo.shape, 0)
    pieces = [t1[0] + ts2_lo, t1[0] + ts2_hi]
    for k1 in range(1, 8):
        pieces.append(jnp.where(sub < PEER_TOPK // (k1 + 1), t1[k1] + ts2_lo, -jnp.inf))
    pieces.append(_stack8(t1[8:]) + t2[0])
    cand = jnp.concatenate(pieces, axis=0)
    best = []
    for _ in range(PEER_TOPK):
        mk = jnp.max(cand, axis=0, keepdims=True)
        cand = jnp.where(cand == mk, -jnp.inf, cand)
        best.append(mk)
    tau = best[-1]
    top = t1[0] + t2[0]
    z = best[0] - top
    z = jnp.exp(z)
    for bk in best[1:]:
        z = z + jnp.exp(bk - top)
    n1 = jnp.zeros(s1.shape, F32)
    for k2 in range(PEER_TOPK // 2):
        n1 = n1 + jnp.where(s1 + t2[k2] >= tau, 1.0, 0.0)
    deep = jnp.zeros(tau.shape, F32)
    for k2 in range(PEER_TOPK // 2, PEER_TOPK):
        deep = deep + jnp.where(t1[0] + t2[k2] >= tau, 1.0, 0.0)
    n1 = n1 + jnp.where(s1 == t1[0], deep, 0.0)
    a1 = jnp.exp(s1 - t1[0]) / z
    b2 = jnp.exp(s2 - t2[0])
    return a1, n1, b2, r2


def _router_kernel(x_ref, g_ref, wq_ref, keys_ref, hn_ref, a1_ref, n1_ref, b2_ref, r2_ref, q_scr,
                   *, tm):
    hn = _rms(x_ref[...], g_ref[...]).astype(BF16)
    hn_ref[...] = hn
    half = PEER_DKEY // 2
    wide = wq_ref.shape[1] // 2
    for part in range(2):
        q = jnp.dot(hn, wq_ref[:, part * wide:(part + 1) * wide], preferred_element_type=F32)
        for c in range(wide // half):
            q_scr[part * (wide // half) + c] = q[:, c * half:(c + 1) * half].astype(BF16)

    def head(h, carry):
        s1 = _dot_nt(keys_ref[2 * h], q_scr[2 * h])
        s2 = _dot_nt(keys_ref[2 * h + 1], q_scr[2 * h + 1])
        for lc in range(tm // LANE):
            ls = slice(lc * LANE, (lc + 1) * LANE)
            a1, n1, b2, r2 = _route_head(s1[:, ls], s2[:, ls])
            a1_ref[h, :, ls] = a1
            n1_ref[h, :, ls] = n1
            b2_ref[h, :, ls] = b2.astype(BF16)
            r2_ref[h, :, ls] = r2.astype(BF16)
        return carry
    lax.fori_loop(0, PEER_HEADS, head, 0)


def _router(x, g, wq, keys, tm=512):
    t, d = x.shape
    tab = pl.BlockSpec((PEER_HEADS, PEER_NKEYS, tm), lambda i: (0, 0, i))
    tab_shape = (PEER_HEADS, PEER_NKEYS, t)
    return pl.pallas_call(
        functools.partial(_router_kernel, tm=tm), grid=(t // tm,),
        in_specs=[pl.BlockSpec((tm, d), lambda i: (i, 0)), _resident((1, d)), _resident(wq.shape),
                  _resident(keys.shape)],
        out_specs=[pl.BlockSpec((tm, d), lambda i: (i, 0)), tab, tab, tab, tab],
        out_shape=[jax.ShapeDtypeStruct((t, d), BF16)]
        + [jax.ShapeDtypeStruct(tab_shape, dt) for dt in (F32, F32, BF16, BF16)],
        scratch_shapes=[pltpu.VMEM((2 * PEER_HEADS, tm, PEER_DKEY // 2), BF16)],
        compiler_params=_cparams(("parallel",)), name="peer_router",
    )(x, g, wq, keys)


def _gelu(x):
    return 0.5 * x * (1.0 + lax.erf(x * (2.0 ** -0.5)))


def _peer_kernel(x_ref, hn_ref, a1_ref, n1_ref, b2_ref, r2_ref, u_ref, vt_ref, y_ref, acc_ref, w_ref,
                 *, eb):
    e = pl.program_id(1)
    tb = hn_ref.shape[0]

    @pl.when(e == 0)
    def _():
        acc_ref[...] = jnp.zeros_like(acc_ref)

    pre = _dot_nt(u_ref[...], hn_ref[...])
    tiles = PEER_NKEYS // BF16_ROWS
    for il in range(eb // PEER_NKEYS):
        g = None
        for h in range(PEER_HEADS):
            n_i = jnp.broadcast_to(n1_ref[h, il:il + 1, :], (BF16_ROWS, tb)).astype(BF16)[None]
            a_i = jnp.broadcast_to(a1_ref[h, il:il + 1, :], (BF16_ROWS, tb)).astype(BF16)[None]
            term = jnp.where(r2_ref[h] < n_i, b2_ref[h], 0.0) * a_i
            g = term if g is None else g + term
        act = _gelu(pre[il * PEER_NKEYS:(il + 1) * PEER_NKEYS, :]).astype(BF16)
        w_ref[il * tiles:(il + 1) * tiles] = act.reshape(tiles, BF16_ROWS, tb) * g
    acc_ref[...] += jnp.dot(vt_ref[0], w_ref[...].reshape(eb, tb), preferred_element_type=F32)

    @pl.when(e == pl.num_programs(1) - 1)
    def _():
        y_ref[...] = x_ref[...] + acc_ref[...].T


def _peer(x, hn, a1, n1, b2, r2, u, vt, tb=512):
    t, d = hn.shape
    nblk, _, eb = vt.shape
    tiles = PEER_NKEYS // BF16_ROWS
    once = dict(pipeline_mode=pl.Buffered(1))
    key1 = pl.BlockSpec((PEER_HEADS, eb // PEER_NKEYS, tb), lambda i, e: (0, e, i))
    key2 = pl.BlockSpec((PEER_HEADS, tiles, BF16_ROWS, tb), lambda i, e: (0, 0, 0, i), **once)
    return pl.pallas_call(
        functools.partial(_peer_kernel, eb=eb), grid=(t // tb, nblk),
        in_specs=[pl.BlockSpec((tb, d), lambda i, e: (i, 0), **once),
                  pl.BlockSpec((tb, d), lambda i, e: (i, 0), **once), key1, key1, key2, key2,
                  pl.BlockSpec((eb, d), lambda i, e: (e, 0)),
                  pl.BlockSpec((1, d, eb), lambda i, e: (e, 0, 0))],
        out_specs=pl.BlockSpec((tb, d), lambda i, e: (i, 0)),
        out_shape=jax.ShapeDtypeStruct((t, d), F32),
        scratch_shapes=[pltpu.VMEM((d, tb), F32),
                        pltpu.VMEM((eb // BF16_ROWS, BF16_ROWS, tb), BF16)],
        compiler_params=_cparams(("parallel", "arbitrary")), name="peer_experts",
    )(x, hn, a1, n1, b2, r2, u, vt)


def _t5_buckets(rel):
    nb = REL_BUCKETS // 2
    max_exact = nb // 2
    n = np.abs(rel)
    large = max_exact + (np.log(np.maximum(n, 1) / max_exact) / np.log(REL_MAX_DIST / max_exact)
                         * (nb - max_exact)).astype(np.int64)
    large = np.minimum(large, nb - 1)
    return np.where(rel > 0, nb, 0) + np.where(n < max_exact, n, large)


def _window_bias(rel_bias):
    span = A_BLOCK + 2 * A_WINDOW
    rel = (np.arange(span)[None, :] - A_WINDOW) - np.arange(A_BLOCK)[:, None]
    bias = jnp.transpose(rel_bias.astype(F32)[_t5_buckets(rel)], (2, 0, 1))
    bias = jnp.where(jnp.asarray(np.abs(rel) <= A_WINDOW)[None], bias, NEG)
    return bias.reshape(A_KV_HEADS, (A_HEADS // A_KV_HEADS) * A_BLOCK, span)


def _neighbourhood_patterns(rows):
    kh = min(NA_MAX_KH, rows)
    span = kh + B_QROWS
    assert rows % B_QROWS == 0 and rows >= span
    m = np.arange(rows // B_QROWS)[:, None, None]
    q_row = m * B_QROWS + np.arange(B_QROWS)[None, :, None]
    k_row = np.clip(m * B_QROWS - kh // 2, 0, rows - span) + np.arange(span)[None, None, :]
    r0 = np.clip(q_row - kh // 2, 0, rows - kh)
    row_ok = (k_row >= r0) & (k_row < r0 + kh)
    dr = np.where(row_ok, k_row - q_row + NA_MAX_KH - 1, 0)
    assert dr.min() >= 0 and dr.max() <= 2 * NA_MAX_KH - 2
    key = np.concatenate([dr.reshape(len(m), -1), row_ok.reshape(len(m), -1)], axis=1)
    _, first_idx, ids = np.unique(key, axis=0, return_index=True, return_inverse=True)
    return ids.reshape(-1).astype(np.int32), dr[first_idx], row_ok[first_idx]


def _neighbourhood_bias(rpb, rows):
    ids, dr, row_ok = _neighbourhood_patterns(rows)
    qc = np.arange(GRID_W)[:, None]
    kc = np.arange(GRID_W)[None, :]
    c_start = np.clip(qc - NA_KW // 2, 0, GRID_W - NA_KW)
    col_ok = (kc >= c_start) & (kc < c_start + NA_KW)
    dc = np.clip(kc - qc + NA_KW - 1, 0, 2 * NA_KW - 2)
    tab = rpb.astype(F32)[:, dr][..., dc]
    ok = row_ok[:, :, :, None, None] & col_ok[None, None, None]
    tab = jnp.where(jnp.asarray(ok)[None], tab, NEG)
    tab = jnp.transpose(tab, (0, 1, 2, 4, 3, 5))
    npat, nqr, nkr = dr.shape
    return tab.reshape(B_HEADS, npat, nqr * GRID_W, nkr * GRID_W), jnp.asarray(ids)


def _rope_tables(seq):
    half = HEAD_DIM // 2
    nf = half // 2
    inv = (ROPE_THETA ** (-np.arange(nf) * 2.0 / half)).astype(np.float32).astype(np.float64)
    t = np.arange(seq)
    pos = np.stack([t // GRID_W, t % GRID_W], axis=1).astype(np.float64)
    ang = pos[:, :, None] * inv[None, None, :]
    cos = np.repeat(np.cos(ang), 2, axis=1).reshape(seq, 2, 2, nf)
    sin = np.repeat(np.sin(ang), 2, axis=1).reshape(seq, 2, 2, nf)
    sin = sin * np.array([-1.0, 1.0])[None, None, :, None]
    cos = np.tile(cos.reshape(seq, HEAD_DIM), (1, CHUNK // HEAD_DIM))
    sin = np.tile(sin.reshape(seq, HEAD_DIM), (1, CHUNK // HEAD_DIM))
    return jnp.asarray(cos, F32), jnp.asarray(sin, F32)


def _permute_cols(w):
    return jnp.concatenate([w[..., _REF_SECTIONS[n][0]:_REF_SECTIONS[n][0] + _REF_SECTIONS[n][1]]
                            for n in _PERM_ORDER], axis=-1)


def _proj_gains(ga, gb, gc):
    scale = HEAD_DIM ** -0.5
    one = jnp.ones((HEAD_DIM,), F32)
    per = dict(qa=(ga[0] * scale, A_HEADS), ka=(ga[1], A_KV_HEADS), va=(one, A_KV_HEADS),
               qb=(gb[0] * scale, B_HEADS), kb=(gb[1], B_HEADS), vb=(one, B_HEADS),
               qc=(gc[0] * scale, C_HEADS), kc=(gc[1], C_KV_HEADS), vc=(one, C_KV_HEADS))
    return jnp.concatenate([jnp.tile(per[n][0].astype(F32), per[n][1]) for n in _PERM_ORDER])[None]


def kernel(x, mem, t5_rel_bias, norm_mix, w_in, qk_norm_a, sink_a, qk_norm_b, rpb_b, qk_norm_c,
           out_norm, w_out, norm_mem, norm_mem_kv, w_mem_q, w_mem_kv, qk_norm_mem, w_mem_o,
           norm_ffn, peer_w_q, peer_keys, peer_u, peer_v):
    b, s, d = x.shape
    depth = w_in.shape[0]
    t = b * s
    cos, sin = _rope_tables(s)
    blockdiag = jnp.asarray(np.kron(np.eye(CHUNK // HEAD_DIM), np.ones((HEAD_DIM, HEAD_DIM))), BF16)
    bias_a = _window_bias(t5_rel_bias)
    xf = x.reshape(t, d)
    for l in range(depth):
        qkv = _inproj(xf, norm_mix[l][None], _permute_cols(w_in[l].astype(BF16)),
                      _proj_gains(qk_norm_a[l], qk_norm_b[l], qk_norm_c[l]), cos, sin,
                      blockdiag, s).reshape(b, s, D_IN)
        g_out = out_norm[l]
        ma = _attn_a(qkv, sink_a[l].astype(F32), bias_a, g_out[None, :A_Q])
        bias_b, ids_b = _neighbourhood_bias(rpb_b[l], s // GRID_W)
        mb = _attn_b(qkv, bias_b, ids_b, g_out[None, A_Q:A_Q + B_W])
        mc = _attn_c(qkv, g_out[None, A_Q + B_W:])
        xf = _outproj(xf, ma.reshape(t, A_Q), mb.reshape(t, B_W), mc.reshape(t, C_Q),
                      w_out[l].astype(BF16))
        km, vm = _memkv(mem, norm_mem_kv[l][None], w_mem_kv[l].astype(BF16), qk_norm_mem[l, 1][None])
        xf = _memattn(xf.reshape(b, s, d), norm_mem[l][None], w_mem_q[l].astype(BF16),
                      qk_norm_mem[l, 0][None], km, vm, w_mem_o[l].astype(BF16)).reshape(t, d)
        keys = peer_keys[l].reshape(2 * PEER_HEADS, PEER_NKEYS, PEER_DKEY // 2).astype(BF16)
        hn, a1, n1, b2, r2 = _router(xf, norm_ffn[l][None], peer_w_q[l].astype(BF16), keys)
        fold = (PEER_HEADS, PEER_NKEYS // BF16_ROWS, BF16_ROWS, t)
        xf = _peer(xf, hn, a1, n1, b2.reshape(fold), r2.reshape(fold),
                   peer_u[l].astype(BF16),
                   peer_v[l].astype(BF16).reshape(-1, PEER_EB, d).transpose(0, 2, 1))
    return xf.reshape(b, s, d)
```

```python
import functools

import numpy as np
import jax
import jax.numpy as jnp
from jax import lax
from jax.experimental import pallas as pl
from jax.experimental.pallas import tpu as pltpu

F32 = jnp.float32
BF16 = jnp.bfloat16

GRID_W = 64
HEAD_DIM = 64
EPS = 1e-6
NEG = -1e30
A_HEADS, A_KV_HEADS, A_WINDOW, A_BLOCK = 12, 4, 128, 128
B_HEADS, NA_MAX_KH, NA_KW = 8, 8, 16
A_QBLOCKS = 2
B_QROWS = 4
C_HEADS, C_KV_HEADS = 12, 4
C_BLOCK = 256
ROPE_THETA = 10000.0
REL_BUCKETS, REL_MAX_DIST = 32, 128
MEM_HEADS, MEM_HEAD_DIM = 4, 128
MEM_W = MEM_HEADS * MEM_HEAD_DIM
PEER_HEADS, PEER_NKEYS, PEER_DKEY, PEER_TOPK = 8, 128, 256, 16
A_Q, A_KV = A_HEADS * HEAD_DIM, A_KV_HEADS * HEAD_DIM
B_W = B_HEADS * HEAD_DIM
C_Q, C_KV = C_HEADS * HEAD_DIM, C_KV_HEADS * HEAD_DIM
D_MIX = A_Q + B_W + C_Q
D_IN = A_Q + 2 * A_KV + 3 * B_W + C_Q + 2 * C_KV

LANE = 128
BF16_ROWS = 16
PEER_EB = 1024
CHUNK = 256
VMEM_LIMIT = 56 * 1024 * 1024

_REF_SECTIONS = dict(qa=(0, A_Q), ka=(A_Q, A_KV), va=(A_Q + A_KV, A_KV),
                     qb=(A_Q + 2 * A_KV, B_W), kb=(A_Q + 2 * A_KV + B_W, B_W),
                     vb=(A_Q + 2 * A_KV + 2 * B_W, B_W),
                     qc=(A_Q + 2 * A_KV + 3 * B_W, C_Q),
                     kc=(A_Q + 2 * A_KV + 3 * B_W + C_Q, C_KV),
                     vc=(A_Q + 2 * A_KV + 3 * B_W + C_Q + C_KV, C_KV))
_PERM_ORDER = ("qa", "qc", "qb", "kb", "vb", "ka", "va", "kc", "vc")
_KIND = dict(qa="norm", qc="rope", qb="norm", kb="norm", vb="plain", ka="norm", va="plain",
             kc="rope", vc="plain")
_PERM_OFF = {}
_off = 0
for _n in _PERM_ORDER:
    _PERM_OFF[_n] = _off
    _off += _REF_SECTIONS[_n][1]
_CHUNK_KINDS = []
for _n in _PERM_ORDER:
    _CHUNK_KINDS += [_KIND[_n]] * (_REF_SECTIONS[_n][1] // CHUNK)


def _cparams(sem):
    return pltpu.CompilerParams(dimension_semantics=sem, vmem_limit_bytes=VMEM_LIMIT)


def _resident(shape):
    nd = len(shape)
    return pl.BlockSpec(shape, lambda *_: (0,) * nd, pipeline_mode=pl.Buffered(1))


def _rms(x, g):
    return x * lax.rsqrt(jnp.mean(x * x, axis=-1, keepdims=True) + EPS) * g


def _pv(p, v):
    half = (p.shape[1] // 2) // LANE * LANE
    return (jnp.dot(p[:, :half], v[:half], preferred_element_type=F32)
            + jnp.dot(p[:, half:], v[half:], preferred_element_type=F32))


def _dot_nt(a, b):
    return lax.dot_general(a, b, (((1,), (1,)), ((), ())), preferred_element_type=F32)


def _inproj_kernel(x_ref, g_ref, w_ref, gain_ref, cos_ref, sin_ref, bd_ref, qkv_ref):
    hn = _rms(x_ref[...], g_ref[...]).astype(BF16)
    lane = lax.broadcasted_iota(jnp.int32, (1, CHUNK), 1)
    lower_half = (lane % 32) < 16
    for c, kind in enumerate(_CHUNK_KINDS):
        cols = slice(c * CHUNK, (c + 1) * CHUNK)
        if c % 2 == 0:
            pair = jnp.dot(hn, w_ref[:, c * CHUNK:(c + 2) * CHUNK], preferred_element_type=F32)
        acc = pair[:, (c % 2) * CHUNK:(c % 2 + 1) * CHUNK]
        if kind != "plain":
            ss = jnp.dot((acc * acc).astype(BF16), bd_ref[...], preferred_element_type=F32)
            acc = acc * lax.rsqrt(ss * (1.0 / HEAD_DIM) + EPS) * gain_ref[:, cols]
        if kind == "rope":
            partner = jnp.where(lower_half, pltpu.roll(acc, CHUNK - 16, 1), pltpu.roll(acc, 16, 1))
            acc = acc * cos_ref[...] + partner * sin_ref[...]
        qkv_ref[:, cols] = acc.astype(BF16)


def _inproj(x, g, w, gain, cos, sin, bd, seq, tm=512):
    t, d = x.shape
    n = w.shape[1]
    nseq = seq // tm
    tab = pl.BlockSpec((tm, CHUNK), lambda i: (i % nseq, 0))
    return pl.pallas_call(
        _inproj_kernel, grid=(t // tm,),
        in_specs=[pl.BlockSpec((tm, d), lambda i: (i, 0)), _resident((1, d)), _resident((d, n)),
                  _resident((1, n)), tab, tab, _resident((CHUNK, CHUNK))],
        out_specs=pl.BlockSpec((tm, n), lambda i: (i, 0)),
        out_shape=jax.ShapeDtypeStruct((t, n), BF16), compiler_params=_cparams(("parallel",)),
        name="inproj")(x, g, w, gain, cos, sin, bd)


def _attn_a_kernel(sink_ref, q_ref, k_ref, v_ref, bias_ref, gout_ref, o_ref):
    nb = k_ref.shape[1] // A_BLOCK
    for sub in range(A_QBLOCKS):
        j = pl.program_id(1) * A_QBLOCKS + sub
        rows = slice(sub * A_BLOCK, (sub + 1) * A_BLOCK)
        o_ref[0, rows, :] = _window_block(sink_ref, q_ref[0, rows, :], k_ref, v_ref, bias_ref,
                                          gout_ref, j, nb).astype(o_ref.dtype)


def _window_block(sink_ref, q, k_ref, v_ref, bias_ref, gout_ref, j, nb):
    def three(ref):
        return jnp.concatenate(
            [ref[0, pl.ds(pl.multiple_of(jnp.clip(j + d, 0, nb - 1) * A_BLOCK, A_BLOCK), A_BLOCK), :]
             for d in (-1, 0, 1)], axis=0)
    k = three(k_ref)
    v = three(v_ref)
    col = lax.broadcasted_iota(jnp.int32, (1, 3 * A_BLOCK), 1)
    first_ok = jnp.where(j > 0, 0, A_BLOCK)
    end_ok = jnp.where(j < nb - 1, 3 * A_BLOCK, 2 * A_BLOCK)
    ok = (col >= first_ok) & (col < end_ok)
    g = A_HEADS // A_KV_HEADS
    row = lax.broadcasted_iota(jnp.int32, (g * A_BLOCK, 1), 0)
    outs = []
    for hk in range(A_KV_HEADS):
        kh = k[:, hk * HEAD_DIM:(hk + 1) * HEAD_DIM]
        vh = v[:, hk * HEAD_DIM:(hk + 1) * HEAD_DIM]
        qs = jnp.concatenate([q[:, (hk * g + gi) * HEAD_DIM:(hk * g + gi + 1) * HEAD_DIM]
                              for gi in range(g)], axis=0)
        s = jnp.full((g * A_BLOCK, 1), sink_ref[hk * g], F32)
        for gi in range(1, g):
            s = jnp.where(row >= gi * A_BLOCK, sink_ref[hk * g + gi], s)
        logits = jnp.where(ok, _dot_nt(qs, kh) + bias_ref[hk], NEG)
        m = jnp.maximum(jnp.max(logits, axis=-1, keepdims=True), s)
        p = jnp.exp(logits - m)
        denom = jnp.sum(p, axis=-1, keepdims=True) + jnp.exp(s - m)
        o = _pv(p.astype(BF16), vh) / denom
        outs += [o[gi * A_BLOCK:(gi + 1) * A_BLOCK] for gi in range(g)]
    return _rms(jnp.concatenate(outs, axis=-1), gout_ref[...])


def _attn_a(qkv, sink, bias, gout):
    b, s, _ = qkv.shape
    tq = A_QBLOCKS * A_BLOCK
    return pl.pallas_call(
        _attn_a_kernel, grid=(b, s // tq),
        in_specs=[pl.BlockSpec(memory_space=pltpu.SMEM),
                  pl.BlockSpec((1, tq, A_Q), lambda bi, j: (bi, j, _PERM_OFF["qa"] // A_Q)),
                  pl.BlockSpec((1, s, A_KV), lambda bi, j: (bi, 0, _PERM_OFF["ka"] // A_KV)),
                  pl.BlockSpec((1, s, A_KV), lambda bi, j: (bi, 0, _PERM_OFF["va"] // A_KV)),
                  _resident(bias.shape), _resident((1, A_Q))],
        out_specs=pl.BlockSpec((1, tq, A_Q), lambda bi, j: (bi, j, 0)),
        out_shape=jax.ShapeDtypeStruct((b, s, A_Q), BF16),
        compiler_params=_cparams(("parallel", "arbitrary")), name="attn_a",
    )(sink, qkv, qkv, qkv, bias, gout)


def _attn_b_kernel(ids_ref, q_ref, k_ref, v_ref, bias_ref, gout_ref, o_ref, *, rows, kh):
    del ids_ref
    m = pl.program_id(1)
    span = kh + B_QROWS
    first = jnp.clip(m * B_QROWS - kh // 2, 0, rows - span)
    start = pl.multiple_of(first * GRID_W, GRID_W)
    q = q_ref[0]
    k = k_ref[0, pl.ds(start, span * GRID_W), :]
    v = v_ref[0, pl.ds(start, span * GRID_W), :]
    outs = []
    for h in range(B_HEADS):
        hs = slice(h * HEAD_DIM, (h + 1) * HEAD_DIM)
        logits = _dot_nt(q[:, hs], k[:, hs]) + bias_ref[h, 0]
        m = jnp.max(logits, axis=-1, keepdims=True)
        p = jnp.exp(logits - m)
        denom = jnp.sum(p, axis=-1, keepdims=True)
        outs.append(_pv(p.astype(BF16), v[:, hs]) / denom)
    o = jnp.concatenate(outs, axis=-1)
    o_ref[0] = _rms(o, gout_ref[...]).astype(o_ref.dtype)


def _attn_b(qkv, bias, bias_ids, gout):
    b, s, _ = qkv.shape
    rows = s // GRID_W
    kh = min(NA_MAX_KH, rows)
    nq, nk = B_QROWS * GRID_W, (kh + B_QROWS) * GRID_W
    grid_spec = pltpu.PrefetchScalarGridSpec(
        num_scalar_prefetch=1, grid=(b, rows // B_QROWS),
        in_specs=[pl.BlockSpec((1, nq, B_W), lambda bi, m, ids: (bi, m, _PERM_OFF["qb"] // B_W)),
                  pl.BlockSpec((1, s, B_W), lambda bi, m, ids: (bi, 0, _PERM_OFF["kb"] // B_W)),
                  pl.BlockSpec((1, s, B_W), lambda bi, m, ids: (bi, 0, _PERM_OFF["vb"] // B_W)),
                  pl.BlockSpec((B_HEADS, 1, nq, nk), lambda bi, m, ids: (0, ids[m], 0, 0)),
                  pl.BlockSpec((1, B_W), lambda bi, m, ids: (0, 0))],
        out_specs=pl.BlockSpec((1, nq, B_W), lambda bi, m, ids: (bi, m, 0)))
    return pl.pallas_call(
        functools.partial(_attn_b_kernel, rows=rows, kh=kh), grid_spec=grid_spec,
        out_shape=jax.ShapeDtypeStruct((b, s, B_W), BF16),
        compiler_params=_cparams(("parallel", "arbitrary")), name="attn_b",
    )(bias_ids, qkv, qkv, qkv, bias, gout)


def _attn_c_kernel(q_ref, k_ref, v_ref, gout_ref, o_ref):
    q = q_ref[0]
    k = k_ref[0]
    v = v_ref[0]
    g = C_HEADS // C_KV_HEADS

    def group_logits(hk):
        kh = k[:, hk * HEAD_DIM:(hk + 1) * HEAD_DIM]
        qs = jnp.concatenate([q[:, (hk * g + gi) * HEAD_DIM:(hk * g + gi + 1) * HEAD_DIM]
                              for gi in range(g)], axis=0)
        return _dot_nt(qs, kh)

    outs = []
    ahead = group_logits(0)
    for hk in range(C_KV_HEADS):
        vh = v[:, hk * HEAD_DIM:(hk + 1) * HEAD_DIM]
        logits = ahead
        if hk + 1 < C_KV_HEADS:
            ahead = group_logits(hk + 1)
        m = jnp.max(logits, axis=-1, keepdims=True)
        p = jnp.exp(logits - m)
        denom = jnp.sum(p, axis=-1, keepdims=True)
        o = _pv(p.astype(BF16), vh) / denom
        outs += [o[gi * C_BLOCK:(gi + 1) * C_BLOCK] for gi in range(g)]
    o = jnp.concatenate(outs, axis=-1)
    o_ref[0] = _rms(o, gout_ref[...]).astype(o_ref.dtype)


def _attn_c(qkv, gout):
    b, s, _ = qkv.shape
    return pl.pallas_call(
        _attn_c_kernel, grid=(b, s // C_BLOCK),
        in_specs=[pl.BlockSpec((1, C_BLOCK, C_Q), lambda bi, j: (bi, j, _PERM_OFF["qc"] // C_Q)),
                  pl.BlockSpec((1, s, C_KV), lambda bi, j: (bi, 0, _PERM_OFF["kc"] // C_KV)),
                  pl.BlockSpec((1, s, C_KV), lambda bi, j: (bi, 0, _PERM_OFF["vc"] // C_KV)),
                  _resident((1, C_Q))],
        out_specs=pl.BlockSpec((1, C_BLOCK, C_Q), lambda bi, j: (bi, j, 0)),
        out_shape=jax.ShapeDtypeStruct((b, s, C_Q), BF16),
        compiler_params=_cparams(("parallel", "arbitrary")), name="attn_c",
    )(qkv, qkv, qkv, gout)


def _outproj_kernel(x_ref, a_ref, b_ref, c_ref, w_ref, o_ref):
    acc = jnp.dot(a_ref[...], w_ref[0:A_Q, :], preferred_element_type=F32)
    acc += jnp.dot(b_ref[...], w_ref[A_Q:A_Q + B_W, :], preferred_element_type=F32)
    acc += jnp.dot(c_ref[...], w_ref[A_Q + B_W:D_MIX, :], preferred_element_type=F32)
    o_ref[...] = x_ref[...] + acc


def _outproj(x, ma, mb, mc, w, tm=512):
    t, d = x.shape

    def row(n):
        return pl.BlockSpec((tm, n), lambda i: (i, 0))
    return pl.pallas_call(
        _outproj_kernel, grid=(t // tm,),
        in_specs=[row(d), row(A_Q), row(B_W), row(C_Q), _resident(w.shape)],
        out_specs=row(d), out_shape=jax.ShapeDtypeStruct((t, d), F32),
        compiler_params=_cparams(("parallel",)), name="outproj",
    )(x, ma, mb, mc, w)


def _memkv_kernel(mem_ref, g_ref, w_ref, gk_ref, k_ref, v_ref):
    mn = _rms(mem_ref[0], g_ref[...]).astype(BF16)
    kv = jnp.dot(mn, w_ref[...], preferred_element_type=F32)
    for h in range(MEM_HEADS):
        hs = slice(h * MEM_HEAD_DIM, (h + 1) * MEM_HEAD_DIM)
        k_ref[0, :, hs] = _rms(kv[:, hs], gk_ref[...]).astype(BF16)
    v_ref[0] = kv[:, MEM_W:].astype(BF16)


def _memkv(mem, g, w, gk):
    b, m, d = mem.shape
    out = pl.BlockSpec((1, m, MEM_W), lambda i: (i, 0, 0))
    return pl.pallas_call(
        _memkv_kernel, grid=(b,),
        in_specs=[pl.BlockSpec((1, m, d), lambda i: (i, 0, 0)), _resident((1, d)),
                  _resident(w.shape), _resident((1, MEM_HEAD_DIM))],
        out_specs=[out, out], out_shape=[jax.ShapeDtypeStruct((b, m, MEM_W), BF16)] * 2,
        compiler_params=_cparams(("parallel",)), name="memkv",
    )(mem, g, w, gk)


def _memattn_kernel(x_ref, g_ref, wq_ref, gq_ref, k_ref, v_ref, wo_ref, o_ref):
    x = x_ref[0]
    hn = _rms(x, g_ref[...]).astype(BF16)
    q = jnp.dot(hn, wq_ref[...], preferred_element_type=F32)
    outs = []
    for h in range(MEM_HEADS):
        hs = slice(h * MEM_HEAD_DIM, (h + 1) * MEM_HEAD_DIM)
        qh = _rms(q[:, hs], gq_ref[...]).astype(BF16)
        logits = _dot_nt(qh, k_ref[0, :, hs]) * (MEM_HEAD_DIM ** -0.5)
        m = jnp.max(logits, axis=-1, keepdims=True)
        p = jnp.exp(logits - m)
        denom = jnp.sum(p, axis=-1, keepdims=True)
        outs.append(_pv(p.astype(BF16), v_ref[0, :, hs]) / denom)
    o = jnp.concatenate(outs, axis=-1).astype(BF16)
    o_ref[0] = x + jnp.dot(o, wo_ref[...], preferred_element_type=F32)


def _memattn(x, g, wq, gq, km, vm, wo, tm=512):
    b, s, d = x.shape
    m = km.shape[1]
    row = pl.BlockSpec((1, tm, d), lambda bi, i: (bi, i, 0))
    kv = pl.BlockSpec((1, m, MEM_W), lambda bi, i: (bi, 0, 0))
    return pl.pallas_call(
        _memattn_kernel, grid=(b, s // tm),
        in_specs=[row, _resident((1, d)), _resident(wq.shape), _resident((1, MEM_HEAD_DIM)), kv, kv,
                  _resident(wo.shape)],
        out_specs=row, out_shape=jax.ShapeDtypeStruct((b, s, d), F32),
        compiler_params=_cparams(("parallel", "arbitrary")), name="memattn",
    )(x, g, wq, gq, km, vm, wo)


def _sorting_network(n):
    def merge(lo, hi, r):
        step = r * 2
        if step < hi - lo:
            yield from merge(lo, hi, step)
            yield from merge(lo + r, hi, step)
            yield from ((i, i + r) for i in range(lo + r, hi - r, step))
        else:
            yield (lo, lo + r)

    def sort(lo, hi):
        if hi - lo >= 1:
            mid = lo + (hi - lo) // 2
            yield from sort(lo, mid)
            yield from sort(mid + 1, hi)
            yield from merge(lo, hi, 1)
    return tuple(sort(0, n - 1))


_SORT16 = _sorting_network(PEER_NKEYS // 8)


def _top_values(s):
    depth = PEER_NKEYS // 8
    v = [s[8 * i:8 * (i + 1)] for i in range(depth)]
    for i, j in _SORT16:
        v[i], v[j] = jnp.maximum(v[i], v[j]), jnp.minimum(v[i], v[j])
    tops = []
    for kk in range(PEER_TOPK):
        mk = jnp.max(v[0], axis=0, keepdims=True)
        tops.append(mk)
        pop = v[0] == mk
        for i in range(PEER_TOPK - 1 - kk):
            v[i] = jnp.where(pop, v[i + 1], v[i])
    return tops


def _rank_among(tops, s):
    rank = jnp.full(s.shape, float(PEER_TOPK), F32)
    for kk in reversed(range(PEER_TOPK)):
        rank = jnp.where(s >= tops[kk], float(kk), rank)
    return rank


def _stack8(rows):
    sub = lax.broadcasted_iota(jnp.int32, (8, rows[0].shape[1]), 0)
    out = jnp.broadcast_to(rows[0], sub.shape)
    for kk in range(1, 8):
        out = jnp.where(sub == kk, rows[kk], out)
    return out


def _route_head(s1, s2):
    t1 = _top_values(s1)
    t2 = _top_values(s2)
    r2 = _rank_among(t2, s2)
    ts2_lo = _stack8(t2[:8])
    ts2_hi = _stack8(t2[8:])
    sub = lax.broadcasted_iota(jnp.int32, ts2_lo.shape, 0)
    pieces = [t1[0] + ts2_lo, t1[0] + ts2_hi]
    for k1 in range(1, 8):
        pieces.append(jnp.where(sub < PEER_TOPK // (k1 + 1), t1[k1] + ts2_lo, -jnp.inf))
    pieces.append(_stack8(t1[8:]) + t2[0])
    cand = jnp.concatenate(pieces, axis=0)
    best = []
    for _ in range(PEER_TOPK):
        mk = jnp.max(cand, axis=0, keepdims=True)
        cand = jnp.where(cand == mk, -jnp.inf, cand)
        best.append(mk)
    tau = best[-1]
    top = t1[0] + t2[0]
    z = best[0] - top
    z = jnp.exp(z)
    for bk in best[1:]:
        z = z + jnp.exp(bk - top)
    n1 = jnp.zeros(s1.shape, F32)
    for k2 in range(PEER_TOPK // 2):
        n1 = n1 + jnp.where(s1 + t2[k2] >= tau, 1.0, 0.0)
    deep = jnp.zeros(tau.shape, F32)
    for k2 in range(PEER_TOPK // 2, PEER_TOPK):
        deep = deep + jnp.where(t1[0] + t2[k2] >= tau, 1.0, 0.0)
    n1 = n1 + jnp.where(s1 == t1[0], deep, 0.0)
    a1 = jnp.exp(s1 - t1[0]) / z
    b2 = jnp.exp(s2 - t2[0])
    return a1, n1, b2, r2


def _router_kernel(x_ref, g_ref, wq_ref, keys_ref, hn_ref, a1_ref, n1_ref, b2_ref, r2_ref, q_scr,
                   *, tm):
    hn = _rms(x_ref[...], g_ref[...]).astype(BF16)
    hn_ref[...] = hn
    half = PEER_DKEY // 2
    wide = wq_ref.shape[1] // 2
    for part in range(2):
        q = jnp.dot(hn, wq_ref[:, part * wide:(part + 1) * wide], preferred_element_type=F32)
        for c in range(wide // half):
            q_scr[part * (wide // half) + c] = q[:, c * half:(c + 1) * half].astype(BF16)

    def head(h, carry):
        s1 = _dot_nt(keys_ref[2 * h], q_scr[2 * h])
        s2 = _dot_nt(keys_ref[2 * h + 1], q_scr[2 * h + 1])
        for lc in range(tm // LANE):
            ls = slice(lc * LANE, (lc + 1) * LANE)
            a1, n1, b2, r2 = _route_head(s1[:, ls], s2[:, ls])
            a1_ref[h, :, ls] = a1
            n1_ref[h, :, ls] = n1
            b2_ref[h, :, ls] = b2.astype(BF16)
            r2_ref[h, :, ls] = r2.astype(BF16)
        return carry
    lax.fori_loop(0, PEER_HEADS, head, 0)


def _router(x, g, wq, keys, tm=512):
    t, d = x.shape
    tab = pl.BlockSpec((PEER_HEADS, PEER_NKEYS, tm), lambda i: (0, 0, i))
    tab_shape = (PEER_HEADS, PEER_NKEYS, t)
    return pl.pallas_call(
        functools.partial(_router_kernel, tm=tm), grid=(t // tm,),
        in_specs=[pl.BlockSpec((tm, d), lambda i: (i, 0)), _resident((1, d)), _resident(wq.shape),
                  _resident(keys.shape)],
        out_specs=[pl.BlockSpec((tm, d), lambda i: (i, 0)), tab, tab, tab, tab],
        out_shape=[jax.ShapeDtypeStruct((t, d), BF16)]
        + [jax.ShapeDtypeStruct(tab_shape, dt) for dt in (F32, F32, BF16, BF16)],
        scratch_shapes=[pltpu.VMEM((2 * PEER_HEADS, tm, PEER_DKEY // 2), BF16)],
        compiler_params=_cparams(("parallel",)), name="peer_router",
    )(x, g, wq, keys)


def _gelu(x):
    return 0.5 * x * (1.0 + lax.erf(x * (2.0 ** -0.5)))


def _peer_kernel(x_ref, hn_ref, a1_ref, n1_ref, b2_ref, r2_ref, u_ref, vt_ref, y_ref, acc_ref, w_ref,
                 *, eb):
    e = pl.program_id(1)
    tb = hn_ref.shape[0]

    @pl.when(e == 0)
    def _():
        acc_ref[...] = jnp.zeros_like(acc_ref)

    pre = _dot_nt(u_ref[...], hn_ref[...])
    tiles = PEER_NKEYS // BF16_ROWS
    for il in range(eb // PEER_NKEYS):
        g = None
        for h in range(PEER_HEADS):
            n_i = jnp.broadcast_to(n1_ref[h, il:il + 1, :], (BF16_ROWS, tb)).astype(BF16)[None]
            a_i = jnp.broadcast_to(a1_ref[h, il:il + 1, :], (BF16_ROWS, tb)).astype(BF16)[None]
            term = jnp.where(r2_ref[h] < n_i, b2_ref[h], 0.0) * a_i
            g = term if g is None else g + term
        act = _gelu(pre[il * PEER_NKEYS:(il + 1) * PEER_NKEYS, :]).astype(BF16)
        w_ref[il * tiles:(il + 1) * tiles] = act.reshape(tiles, BF16_ROWS, tb) * g
    acc_ref[...] += jnp.dot(vt_ref[0], w_ref[...].reshape(eb, tb), preferred_element_type=F32)

    @pl.when(e == pl.num_programs(1) - 1)
    def _():
        y_ref[...] = x_ref[...] + acc_ref[...].T


def _peer(x, hn, a1, n1, b2, r2, u, vt, tb=512):
    t, d = hn.shape
    nblk, _, eb = vt.shape
    tiles = PEER_NKEYS // BF16_ROWS
    once = dict(pipeline_mode=pl.Buffered(1))
    key1 = pl.BlockSpec((PEER_HEADS, eb // PEER_NKEYS, tb), lambda i, e: (0, e, i))
    key2 = pl.BlockSpec((PEER_HEADS, tiles, BF16_ROWS, tb), lambda i, e: (0, 0, 0, i), **once)
    return pl.pallas_call(
        functools.partial(_peer_kernel, eb=eb), grid=(t // tb, nblk),
        in_specs=[pl.BlockSpec((tb, d), lambda i, e: (i, 0), **once),
                  pl.BlockSpec((tb, d), lambda i, e: (i, 0), **once), key1, key1, key2, key2,
                  pl.BlockSpec((eb, d), lambda i, e: (e, 0)),
                  pl.BlockSpec((1, d, eb), lambda i, e: (e, 0, 0))],
        out_specs=pl.BlockSpec((tb, d), lambda i, e: (i, 0)),
        out_shape=jax.ShapeDtypeStruct((t, d), F32),
        scratch_shapes=[pltpu.VMEM((d, tb), F32),
                        pltpu.VMEM((eb // BF16_ROWS, BF16_ROWS, tb), BF16)],
        compiler_params=_cparams(("parallel", "arbitrary")), name="peer_experts",
    )(x, hn, a1, n1, b2, r2, u, vt)


def _t5_buckets(rel):
    nb = REL_BUCKETS // 2
    max_exact = nb // 2
    n = np.abs(rel)
    large = max_exact + (np.log(np.maximum(n, 1) / max_exact) / np.log(REL_MAX_DIST / max_exact)
                         * (nb - max_exact)).astype(np.int64)
    large = np.minimum(large, nb - 1)
    return np.where(rel > 0, nb, 0) + np.where(n < max_exact, n, large)


def _lookup(table, idx, n):
    onehot = jnp.asarray(np.arange(n)[:, None] == np.asarray(idx)[None, :], F32)
    return jnp.einsum("...d,dn->...n", table, onehot, precision=lax.Precision.HIGHEST)


def _window_bias(rel_bias):
    span = A_BLOCK + 2 * A_WINDOW
    rel = (np.arange(span)[None, :] - A_WINDOW) - np.arange(A_BLOCK)[:, None]
    bias = _lookup(rel_bias.astype(F32).T, _t5_buckets(rel).reshape(-1), REL_BUCKETS)
    bias = bias.reshape(A_HEADS, A_BLOCK, span)
    bias = jnp.where(jnp.asarray(np.abs(rel) <= A_WINDOW)[None], bias, NEG)
    return bias.reshape(A_KV_HEADS, (A_HEADS // A_KV_HEADS) * A_BLOCK, span)


def _neighbourhood_patterns(rows):
    kh = min(NA_MAX_KH, rows)
    span = kh + B_QROWS
    assert rows % B_QROWS == 0 and rows >= span
    m = np.arange(rows // B_QROWS)[:, None, None]
    q_row = m * B_QROWS + np.arange(B_QROWS)[None, :, None]
    k_row = np.clip(m * B_QROWS - kh // 2, 0, rows - span) + np.arange(span)[None, None, :]
    r0 = np.clip(q_row - kh // 2, 0, rows - kh)
    row_ok = (k_row >= r0) & (k_row < r0 + kh)
    dr = np.where(row_ok, k_row - q_row + NA_MAX_KH - 1, 0)
    assert dr.min() >= 0 and dr.max() <= 2 * NA_MAX_KH - 2
    key = np.concatenate([dr.reshape(len(m), -1), row_ok.reshape(len(m), -1)], axis=1)
    _, first_idx, ids = np.unique(key, axis=0, return_index=True, return_inverse=True)
    return ids.reshape(-1).astype(np.int32), dr[first_idx], row_ok[first_idx]


def _neighbourhood_bias(rpb, rows):
    ids, dr, row_ok = _neighbourhood_patterns(rows)
    qc = np.arange(GRID_W)[:, None]
    kc = np.arange(GRID_W)[None, :]
    c_start = np.clip(qc - NA_KW // 2, 0, GRID_W - NA_KW)
    col_ok = (kc >= c_start) & (kc < c_start + NA_KW)
    dc = np.clip(kc - qc + NA_KW - 1, 0, 2 * NA_KW - 2)
    tab = _lookup(rpb.astype(F32)[:, dr], dc.reshape(-1), 2 * NA_KW - 1)
    tab = tab.reshape(tab.shape[:-1] + dc.shape)
    ok = row_ok[:, :, :, None, None] & col_ok[None, None, None]
    tab = jnp.where(jnp.asarray(ok)[None], tab, NEG)
    tab = jnp.transpose(tab, (0, 1, 2, 4, 3, 5))
    npat, nqr, nkr = dr.shape
    return tab.reshape(B_HEADS, npat, nqr * GRID_W, nkr * GRID_W), jnp.asarray(ids)


def _rope_tables(seq):
    half = HEAD_DIM // 2
    nf = half // 2
    inv = (ROPE_THETA ** (-np.arange(nf) * 2.0 / half)).astype(np.float32).astype(np.float64)
    t = np.arange(seq)
    pos = np.stack([t // GRID_W, t % GRID_W], axis=1).astype(np.float64)
    ang = pos[:, :, None] * inv[None, None, :]
    cos = np.repeat(np.cos(ang), 2, axis=1).reshape(seq, 2, 2, nf)
    sin = np.repeat(np.sin(ang), 2, axis=1).reshape(seq, 2, 2, nf)
    sin = sin * np.array([-1.0, 1.0])[None, None, :, None]
    cos = np.tile(cos.reshape(seq, HEAD_DIM), (1, CHUNK // HEAD_DIM))
    sin = np.tile(sin.reshape(seq, HEAD_DIM), (1, CHUNK // HEAD_DIM))
    return jnp.asarray(cos, F32), jnp.asarray(sin, F32)


def _permute_cols(w):
    return jnp.concatenate([w[..., _REF_SECTIONS[n][0]:_REF_SECTIONS[n][0] + _REF_SECTIONS[n][1]]
                            for n in _PERM_ORDER], axis=-1)


def _proj_gains(ga, gb, gc):
    scale = HEAD_DIM ** -0.5
    one = jnp.ones((HEAD_DIM,), F32)
    per = dict(qa=(ga[0] * scale, A_HEADS), ka=(ga[1], A_KV_HEADS), va=(one, A_KV_HEADS),
               qb=(gb[0] * scale, B_HEADS), kb=(gb[1], B_HEADS), vb=(one, B_HEADS),
               qc=(gc[0] * scale, C_HEADS), kc=(gc[1], C_KV_HEADS), vc=(one, C_KV_HEADS))
    return jnp.concatenate([jnp.tile(per[n][0].astype(F32), per[n][1]) for n in _PERM_ORDER])[None]


def kernel(x, mem, t5_rel_bias, norm_mix, w_in, qk_norm_a, sink_a, qk_norm_b, rpb_b, qk_norm_c,
           out_norm, w_out, norm_mem, norm_mem_kv, w_mem_q, w_mem_kv, qk_norm_mem, w_mem_o,
           norm_ffn, peer_w_q, peer_keys, peer_u, peer_v):
    b, s, d = x.shape
    depth = w_in.shape[0]
    t = b * s
    cos, sin = _rope_tables(s)
    blockdiag = jnp.asarray(np.kron(np.eye(CHUNK // HEAD_DIM), np.ones((HEAD_DIM, HEAD_DIM))), BF16)
    bias_a = _window_bias(t5_rel_bias)
    xf = x.reshape(t, d)
    for l in range(depth):
        qkv = _inproj(xf, norm_mix[l][None], _permute_cols(w_in[l].astype(BF16)),
                      _proj_gains(qk_norm_a[l], qk_norm_b[l], qk_norm_c[l]), cos, sin,
                      blockdiag, s).reshape(b, s, D_IN)
        g_out = out_norm[l]
        ma = _attn_a(qkv, sink_a[l].astype(F32), bias_a, g_out[None, :A_Q])
        bias_b, ids_b = _neighbourhood_bias(rpb_b[l], s // GRID_W)
        mb = _attn_b(qkv, bias_b, ids_b, g_out[None, A_Q:A_Q + B_W])
        mc = _attn_c(qkv, g_out[None, A_Q + B_W:])
        xf = _outproj(xf, ma.reshape(t, A_Q), mb.reshape(t, B_W), mc.reshape(t, C_Q),
                      w_out[l].astype(BF16))
        km, vm = _memkv(mem, norm_mem_kv[l][None], w_mem_kv[l].astype(BF16), qk_norm_mem[l, 1][None])
        xf = _memattn(xf.reshape(b, s, d), norm_mem[l][None], w_mem_q[l].astype(BF16),
                      qk_norm_mem[l, 0][None], km, vm, w_mem_o[l].astype(BF16)).reshape(t, d)
        keys = peer_keys[l].reshape(2 * PEER_HEADS, PEER_NKEYS, PEER_DKEY // 2).astype(BF16)
        hn, a1, n1, b2, r2 = _router(xf, norm_ffn[l][None], peer_w_q[l].astype(BF16), keys)
        fold = (PEER_HEADS, PEER_NKEYS // BF16_ROWS, BF16_ROWS, t)
        xf = _peer(xf, hn, a1, n1, b2.reshape(fold), r2.reshape(fold),
                   peer_u[l].astype(BF16),
                   peer_v[l].astype(BF16).reshape(-1, PEER_EB, d).transpose(0, 2, 1))
    return xf.reshape(b, s, d)
```

```python
import functools

import numpy as np
import jax
import jax.numpy as jnp
from jax import lax
from jax.experimental import pallas as pl
from jax.experimental.pallas import tpu as pltpu

F32 = jnp.float32
BF16 = jnp.bfloat16

GRID_W = 64
HEAD_DIM = 64
EPS = 1e-6
NEG = -1e30
A_HEADS, A_KV_HEADS, A_WINDOW, A_BLOCK = 12, 4, 128, 128
B_HEADS, NA_MAX_KH, NA_KW = 8, 8, 16
A_QBLOCKS = 2
B_QROWS = 4
C_HEADS, C_KV_HEADS = 12, 4
C_BLOCK = 256
ROPE_THETA = 10000.0
REL_BUCKETS, REL_MAX_DIST = 32, 128
MEM_HEADS, MEM_HEAD_DIM = 4, 128
MEM_W = MEM_HEADS * MEM_HEAD_DIM
PEER_HEADS, PEER_NKEYS, PEER_DKEY, PEER_TOPK = 8, 128, 256, 16
A_Q, A_KV = A_HEADS * HEAD_DIM, A_KV_HEADS * HEAD_DIM
B_W = B_HEADS * HEAD_DIM
C_Q, C_KV = C_HEADS * HEAD_DIM, C_KV_HEADS * HEAD_DIM
D_MIX = A_Q + B_W + C_Q
D_IN = A_Q + 2 * A_KV + 3 * B_W + C_Q + 2 * C_KV

LANE = 128
BF16_ROWS = 16
PEER_EB = 1024
PEER_KEY_CHUNKS = 4
CHUNK = 256
VMEM_LIMIT = 56 * 1024 * 1024

_REF_SECTIONS = dict(qa=(0, A_Q), ka=(A_Q, A_KV), va=(A_Q + A_KV, A_KV),
                     qb=(A_Q + 2 * A_KV, B_W), kb=(A_Q + 2 * A_KV + B_W, B_W),
                     vb=(A_Q + 2 * A_KV + 2 * B_W, B_W),
                     qc=(A_Q + 2 * A_KV + 3 * B_W, C_Q),
                     kc=(A_Q + 2 * A_KV + 3 * B_W + C_Q, C_KV),
                     vc=(A_Q + 2 * A_KV + 3 * B_W + C_Q + C_KV, C_KV))
_PERM_ORDER = ("qa", "qc", "qb", "kb", "vb", "ka", "va", "kc", "vc")
_KIND = dict(qa="norm", qc="rope", qb="norm", kb="norm", vb="plain", ka="norm", va="plain",
             kc="rope", vc="plain")
_PERM_OFF = {}
_off = 0
for _n in _PERM_ORDER:
    _PERM_OFF[_n] = _off
    _off += _REF_SECTIONS[_n][1]
_CHUNK_KINDS = []
for _n in _PERM_ORDER:
    _CHUNK_KINDS += [_KIND[_n]] * (_REF_SECTIONS[_n][1] // CHUNK)


def _cparams(sem):
    return pltpu.CompilerParams(dimension_semantics=sem, vmem_limit_bytes=VMEM_LIMIT)


def _resident(shape):
    nd = len(shape)
    return pl.BlockSpec(shape, lambda *_: (0,) * nd, pipeline_mode=pl.Buffered(1))


def _rms(x, g):
    return x * lax.rsqrt(jnp.mean(x * x, axis=-1, keepdims=True) + EPS) * g


def _pv(p, v):
    half = (p.shape[1] // 2) // LANE * LANE
    return (jnp.dot(p[:, :half], v[:half], preferred_element_type=F32)
            + jnp.dot(p[:, half:], v[half:], preferred_element_type=F32))


def _dot_nt(a, b):
    return lax.dot_general(a, b, (((1,), (1,)), ((), ())), preferred_element_type=F32)


def _inproj_kernel(x_ref, g_ref, w_ref, gain_ref, cos_ref, sin_ref, bd_ref, qkv_ref):
    hn = _rms(x_ref[...], g_ref[...]).astype(BF16)
    lane = lax.broadcasted_iota(jnp.int32, (1, CHUNK), 1)
    lower_half = (lane % 32) < 16
    for c, kind in enumerate(_CHUNK_KINDS):
        cols = slice(c * CHUNK, (c + 1) * CHUNK)
        if c % 2 == 0:
            pair = jnp.dot(hn, w_ref[:, c * CHUNK:(c + 2) * CHUNK], preferred_element_type=F32)
        acc = pair[:, (c % 2) * CHUNK:(c % 2 + 1) * CHUNK]
        if kind != "plain":
            ss = jnp.dot((acc * acc).astype(BF16), bd_ref[...], preferred_element_type=F32)
            acc = acc * lax.rsqrt(ss * (1.0 / HEAD_DIM) + EPS) * gain_ref[:, cols]
        if kind == "rope":
            partner = jnp.where(lower_half, pltpu.roll(acc, CHUNK - 16, 1), pltpu.roll(acc, 16, 1))
            acc = acc * cos_ref[...] + partner * sin_ref[...]
        qkv_ref[:, cols] = acc.astype(BF16)


def _inproj(x, g, w, gain, cos, sin, bd, seq, tm=512):
    t, d = x.shape
    n = w.shape[1]
    nseq = seq // tm
    tab = pl.BlockSpec((tm, CHUNK), lambda i: (i % nseq, 0))
    return pl.pallas_call(
        _inproj_kernel, grid=(t // tm,),
        in_specs=[pl.BlockSpec((tm, d), lambda i: (i, 0)), _resident((1, d)), _resident((d, n)),
                  _resident((1, n)), tab, tab, _resident((CHUNK, CHUNK))],
        out_specs=pl.BlockSpec((tm, n), lambda i: (i, 0)),
        out_shape=jax.ShapeDtypeStruct((t, n), BF16), compiler_params=_cparams(("parallel",)),
        name="inproj")(x, g, w, gain, cos, sin, bd)


def _attn_a_kernel(sink_ref, q_ref, k_ref, v_ref, bias_ref, gout_ref, o_ref):
    nb = k_ref.shape[1] // A_BLOCK
    for sub in range(A_QBLOCKS):
        j = pl.program_id(1) * A_QBLOCKS + sub
        rows = slice(sub * A_BLOCK, (sub + 1) * A_BLOCK)
        o_ref[0, rows, :] = _window_block(sink_ref, q_ref[0, rows, :], k_ref, v_ref, bias_ref,
                                          gout_ref, j, nb).astype(o_ref.dtype)


def _window_block(sink_ref, q, k_ref, v_ref, bias_ref, gout_ref, j, nb):
    def three(ref):
        return jnp.concatenate(
            [ref[0, pl.ds(pl.multiple_of(jnp.clip(j + d, 0, nb - 1) * A_BLOCK, A_BLOCK), A_BLOCK), :]
             for d in (-1, 0, 1)], axis=0)
    k = three(k_ref)
    v = three(v_ref)
    col = lax.broadcasted_iota(jnp.int32, (1, 3 * A_BLOCK), 1)
    first_ok = jnp.where(j > 0, 0, A_BLOCK)
    end_ok = jnp.where(j < nb - 1, 3 * A_BLOCK, 2 * A_BLOCK)
    ok = (col >= first_ok) & (col < end_ok)
    g = A_HEADS // A_KV_HEADS
    row = lax.broadcasted_iota(jnp.int32, (g * A_BLOCK, 1), 0)
    outs = []
    for hk in range(A_KV_HEADS):
        kh = k[:, hk * HEAD_DIM:(hk + 1) * HEAD_DIM]
        vh = v[:, hk * HEAD_DIM:(hk + 1) * HEAD_DIM]
        qs = jnp.concatenate([q[:, (hk * g + gi) * HEAD_DIM:(hk * g + gi + 1) * HEAD_DIM]
                              for gi in range(g)], axis=0)
        s = jnp.full((g * A_BLOCK, 1), sink_ref[hk * g], F32)
        for gi in range(1, g):
            s = jnp.where(row >= gi * A_BLOCK, sink_ref[hk * g + gi], s)
        logits = jnp.where(ok, _dot_nt(qs, kh) + bias_ref[hk], NEG)
        m = jnp.maximum(jnp.max(logits, axis=-1, keepdims=True), s)
        p = jnp.exp(logits - m)
        denom = jnp.sum(p, axis=-1, keepdims=True) + jnp.exp(s - m)
        o = _pv(p.astype(BF16), vh) / denom
        outs += [o[gi * A_BLOCK:(gi + 1) * A_BLOCK] for gi in range(g)]
    return _rms(jnp.concatenate(outs, axis=-1), gout_ref[...])


def _attn_a(qkv, sink, bias, gout):
    b, s, _ = qkv.shape
    tq = A_QBLOCKS * A_BLOCK
    return pl.pallas_call(
        _attn_a_kernel, grid=(b, s // tq),
        in_specs=[pl.BlockSpec(memory_space=pltpu.SMEM),
                  pl.BlockSpec((1, tq, A_Q), lambda bi, j: (bi, j, _PERM_OFF["qa"] // A_Q)),
                  pl.BlockSpec((1, s, A_KV), lambda bi, j: (bi, 0, _PERM_OFF["ka"] // A_KV)),
                  pl.BlockSpec((1, s, A_KV), lambda bi, j: (bi, 0, _PERM_OFF["va"] // A_KV)),
                  _resident(bias.shape), _resident((1, A_Q))],
        out_specs=pl.BlockSpec((1, tq, A_Q), lambda bi, j: (bi, j, 0)),
        out_shape=jax.ShapeDtypeStruct((b, s, A_Q), BF16),
        compiler_params=_cparams(("parallel", "arbitrary")), name="attn_a",
    )(sink, qkv, qkv, qkv, bias, gout)


def _attn_b_kernel(ids_ref, q_ref, k_ref, v_ref, bias_ref, gout_ref, o_ref, *, rows, kh):
    del ids_ref
    m = pl.program_id(1)
    span = kh + B_QROWS
    first = jnp.clip(m * B_QROWS - kh // 2, 0, rows - span)
    start = pl.multiple_of(first * GRID_W, GRID_W)
    q = q_ref[0]
    k = k_ref[0, pl.ds(start, span * GRID_W), :]
    v = v_ref[0, pl.ds(start, span * GRID_W), :]
    outs = []
    for h in range(B_HEADS):
        hs = slice(h * HEAD_DIM, (h + 1) * HEAD_DIM)
        logits = _dot_nt(q[:, hs], k[:, hs]) + bias_ref[h, 0]
        m = jnp.max(logits, axis=-1, keepdims=True)
        p = jnp.exp(logits - m)
        denom = jnp.sum(p, axis=-1, keepdims=True)
        outs.append(_pv(p.astype(BF16), v[:, hs]) / denom)
    o = jnp.concatenate(outs, axis=-1)
    o_ref[0] = _rms(o, gout_ref[...]).astype(o_ref.dtype)


def _attn_b(qkv, bias, bias_ids, gout):
    b, s, _ = qkv.shape
    rows = s // GRID_W
    kh = min(NA_MAX_KH, rows)
    nq, nk = B_QROWS * GRID_W, (kh + B_QROWS) * GRID_W
    grid_spec = pltpu.PrefetchScalarGridSpec(
        num_scalar_prefetch=1, grid=(b, rows // B_QROWS),
        in_specs=[pl.BlockSpec((1, nq, B_W), lambda bi, m, ids: (bi, m, _PERM_OFF["qb"] // B_W)),
                  pl.BlockSpec((1, s, B_W), lambda bi, m, ids: (bi, 0, _PERM_OFF["kb"] // B_W)),
                  pl.BlockSpec((1, s, B_W), lambda bi, m, ids: (bi, 0, _PERM_OFF["vb"] // B_W)),
                  pl.BlockSpec((B_HEADS, 1, nq, nk), lambda bi, m, ids: (0, ids[m], 0, 0)),
                  pl.BlockSpec((1, B_W), lambda bi, m, ids: (0, 0))],
        out_specs=pl.BlockSpec((1, nq, B_W), lambda bi, m, ids: (bi, m, 0)))
    return pl.pallas_call(
        functools.partial(_attn_b_kernel, rows=rows, kh=kh), grid_spec=grid_spec,
        out_shape=jax.ShapeDtypeStruct((b, s, B_W), BF16),
        compiler_params=_cparams(("parallel", "arbitrary")), name="attn_b",
    )(bias_ids, qkv, qkv, qkv, bias, gout)


def _attn_c_kernel(q_ref, k_ref, v_ref, gout_ref, o_ref):
    q = q_ref[0]
    k = k_ref[0]
    v = v_ref[0]
    g = C_HEADS // C_KV_HEADS

    def group_logits(hk):
        kh = k[:, hk * HEAD_DIM:(hk + 1) * HEAD_DIM]
        qs = jnp.concatenate([q[:, (hk * g + gi) * HEAD_DIM:(hk * g + gi + 1) * HEAD_DIM]
                              for gi in range(g)], axis=0)
        return _dot_nt(qs, kh)

    outs = []
    ahead = group_logits(0)
    for hk in range(C_KV_HEADS):
        vh = v[:, hk * HEAD_DIM:(hk + 1) * HEAD_DIM]
        logits = ahead
        if hk + 1 < C_KV_HEADS:
            ahead = group_logits(hk + 1)
        m = jnp.max(logits, axis=-1, keepdims=True)
        p = jnp.exp(logits - m)
        denom = jnp.sum(p, axis=-1, keepdims=True)
        o = _pv(p.astype(BF16), vh) / denom
        outs += [o[gi * C_BLOCK:(gi + 1) * C_BLOCK] for gi in range(g)]
    o = jnp.concatenate(outs, axis=-1)
    o_ref[0] = _rms(o, gout_ref[...]).astype(o_ref.dtype)


def _attn_c(qkv, gout):
    b, s, _ = qkv.shape
    return pl.pallas_call(
        _attn_c_kernel, grid=(b, s // C_BLOCK),
        in_specs=[pl.BlockSpec((1, C_BLOCK, C_Q), lambda bi, j: (bi, j, _PERM_OFF["qc"] // C_Q)),
                  pl.BlockSpec((1, s, C_KV), lambda bi, j: (bi, 0, _PERM_OFF["kc"] // C_KV)),
                  pl.BlockSpec((1, s, C_KV), lambda bi, j: (bi, 0, _PERM_OFF["vc"] // C_KV)),
                  _resident((1, C_Q))],
        out_specs=pl.BlockSpec((1, C_BLOCK, C_Q), lambda bi, j: (bi, j, 0)),
        out_shape=jax.ShapeDtypeStruct((b, s, C_Q), BF16),
        compiler_params=_cparams(("parallel", "arbitrary")), name="attn_c",
    )(qkv, qkv, qkv, gout)


def _outproj_kernel(x_ref, a_ref, b_ref, c_ref, w_ref, o_ref):
    acc = jnp.dot(a_ref[...], w_ref[0:A_Q, :], preferred_element_type=F32)
    acc += jnp.dot(b_ref[...], w_ref[A_Q:A_Q + B_W, :], preferred_element_type=F32)
    acc += jnp.dot(c_ref[...], w_ref[A_Q + B_W:D_MIX, :], preferred_element_type=F32)
    o_ref[...] = x_ref[...] + acc


def _outproj(x, ma, mb, mc, w, tm=512):
    t, d = x.shape

    def row(n):
        return pl.BlockSpec((tm, n), lambda i: (i, 0))
    return pl.pallas_call(
        _outproj_kernel, grid=(t // tm,),
        in_specs=[row(d), row(A_Q), row(B_W), row(C_Q), _resident(w.shape)],
        out_specs=row(d), out_shape=jax.ShapeDtypeStruct((t, d), F32),
        compiler_params=_cparams(("parallel",)), name="outproj",
    )(x, ma, mb, mc, w)


def _memkv_kernel(mem_ref, g_ref, w_ref, gk_ref, k_ref, v_ref):
    mn = _rms(mem_ref[0], g_ref[...]).astype(BF16)
    kv = jnp.dot(mn, w_ref[...], preferred_element_type=F32)
    for h in range(MEM_HEADS):
        hs = slice(h * MEM_HEAD_DIM, (h + 1) * MEM_HEAD_DIM)
        k_ref[0, :, hs] = _rms(kv[:, hs], gk_ref[...]).astype(BF16)
    v_ref[0] = kv[:, MEM_W:].astype(BF16)


def _memkv(mem, g, w, gk):
    b, m, d = mem.shape
    out = pl.BlockSpec((1, m, MEM_W), lambda i: (i, 0, 0))
    return pl.pallas_call(
        _memkv_kernel, grid=(b,),
        in_specs=[pl.BlockSpec((1, m, d), lambda i: (i, 0, 0)), _resident((1, d)),
                  _resident(w.shape), _resident((1, MEM_HEAD_DIM))],
        out_specs=[out, out], out_shape=[jax.ShapeDtypeStruct((b, m, MEM_W), BF16)] * 2,
        compiler_params=_cparams(("parallel",)), name="memkv",
    )(mem, g, w, gk)


def _memattn_kernel(x_ref, g_ref, wq_ref, gq_ref, k_ref, v_ref, wo_ref, o_ref):
    x = x_ref[0]
    hn = _rms(x, g_ref[...]).astype(BF16)
    q = jnp.dot(hn, wq_ref[...], preferred_element_type=F32)
    outs = []
    for h in range(MEM_HEADS):
        hs = slice(h * MEM_HEAD_DIM, (h + 1) * MEM_HEAD_DIM)
        qh = _rms(q[:, hs], gq_ref[...]).astype(BF16)
        logits = _dot_nt(qh, k_ref[0, :, hs]) * (MEM_HEAD_DIM ** -0.5)
        m = jnp.max(logits, axis=-1, keepdims=True)
        p = jnp.exp(logits - m)
        denom = jnp.sum(p, axis=-1, keepdims=True)
        outs.append(_pv(p.astype(BF16), v_ref[0, :, hs]) / denom)
    o = jnp.concatenate(outs, axis=-1).astype(BF16)
    o_ref[0] = x + jnp.dot(o, wo_ref[...], preferred_element_type=F32)


def _memattn(x, g, wq, gq, km, vm, wo, tm=512):
    b, s, d = x.shape
    m = km.shape[1]
    row = pl.BlockSpec((1, tm, d), lambda bi, i: (bi, i, 0))
    kv = pl.BlockSpec((1, m, MEM_W), lambda bi, i: (bi, 0, 0))
    return pl.pallas_call(
        _memattn_kernel, grid=(b, s // tm),
        in_specs=[row, _resident((1, d)), _resident(wq.shape), _resident((1, MEM_HEAD_DIM)), kv, kv,
                  _resident(wo.shape)],
        out_specs=row, out_shape=jax.ShapeDtypeStruct((b, s, d), F32),
        compiler_params=_cparams(("parallel", "arbitrary")), name="memattn",
    )(x, g, wq, gq, km, vm, wo)


def _sorting_network(n):
    def merge(lo, hi, r):
        step = r * 2
        if step < hi - lo:
            yield from merge(lo, hi, step)
            yield from merge(lo + r, hi, step)
            yield from ((i, i + r) for i in range(lo + r, hi - r, step))
        else:
            yield (lo, lo + r)

    def sort(lo, hi):
        if hi - lo >= 1:
            mid = lo + (hi - lo) // 2
            yield from sort(lo, mid)
            yield from sort(mid + 1, hi)
            yield from merge(lo, hi, 1)
    return tuple(sort(0, n - 1))


_SORT16 = _sorting_network(PEER_NKEYS // 8)


def _top_values(s):
    depth = PEER_NKEYS // 8
    v = [s[8 * i:8 * (i + 1)] for i in range(depth)]
    for i, j in _SORT16:
        v[i], v[j] = jnp.maximum(v[i], v[j]), jnp.minimum(v[i], v[j])
    tops = []
    for kk in range(PEER_TOPK):
        mk = jnp.max(v[0], axis=0, keepdims=True)
        tops.append(mk)
        pop = v[0] == mk
        for i in range(PEER_TOPK - 1 - kk):
            v[i] = jnp.where(pop, v[i + 1], v[i])
    return tops


def _rank_among(tops, s):
    rank = jnp.full(s.shape, float(PEER_TOPK), F32)
    for kk in reversed(range(PEER_TOPK)):
        rank = jnp.where(s >= tops[kk], float(kk), rank)
    return rank


def _stack8(rows):
    sub = lax.broadcasted_iota(jnp.int32, (8, rows[0].shape[1]), 0)
    out = jnp.broadcast_to(rows[0], sub.shape)
    for kk in range(1, 8):
        out = jnp.where(sub == kk, rows[kk], out)
    return out


def _route_head(s1, s2):
    t1 = _top_values(s1)
    t2 = _top_values(s2)
    r2 = _rank_among(t2, s2)
    ts2_lo = _stack8(t2[:8])
    ts2_hi = _stack8(t2[8:])
    sub = lax.broadcasted_iota(jnp.int32, ts2_lo.shape, 0)
    pieces = [t1[0] + ts2_lo, t1[0] + ts2_hi]
    for k1 in range(1, 8):
        pieces.append(jnp.where(sub < PEER_TOPK // (k1 + 1), t1[k1] + ts2_lo, -jnp.inf))
    pieces.append(_stack8(t1[8:]) + t2[0])
    cand = jnp.concatenate(pieces, axis=0)
    best = []
    for _ in range(PEER_TOPK):
        mk = jnp.max(cand, axis=0, keepdims=True)
        cand = jnp.where(cand == mk, -jnp.inf, cand)
        best.append(mk)
    tau = best[-1]
    top = t1[0] + t2[0]
    z = best[0] - top
    z = jnp.exp(z)
    for bk in best[1:]:
        z = z + jnp.exp(bk - top)
    n1 = jnp.zeros(s1.shape, F32)
    for k2 in range(PEER_TOPK // 2):
        n1 = n1 + jnp.where(s1 + t2[k2] >= tau, 1.0, 0.0)
    deep = jnp.zeros(tau.shape, F32)
    for k2 in range(PEER_TOPK // 2, PEER_TOPK):
        deep = deep + jnp.where(t1[0] + t2[k2] >= tau, 1.0, 0.0)
    n1 = n1 + jnp.where(s1 == t1[0], deep, 0.0)
    a1 = jnp.exp(s1 - t1[0]) / z
    b2 = jnp.exp(s2 - t2[0])
    return a1, n1, b2, r2


def _router_kernel(x_ref, g_ref, wq_ref, keys_ref, hn_ref, a1_ref, n1_ref, b2_ref, r2_ref, q_scr,
                   *, tm):
    hn = _rms(x_ref[...], g_ref[...]).astype(BF16)
    hn_ref[...] = hn
    half = PEER_DKEY // 2
    wide = wq_ref.shape[1] // 2
    for part in range(2):
        q = jnp.dot(hn, wq_ref[:, part * wide:(part + 1) * wide], preferred_element_type=F32)
        for c in range(wide // half):
            q_scr[part * (wide // half) + c] = q[:, c * half:(c + 1) * half].astype(BF16)

    def head(h, carry):
        s1 = _dot_nt(keys_ref[2 * h], q_scr[2 * h])
        s2 = _dot_nt(keys_ref[2 * h + 1], q_scr[2 * h + 1])
        for lc in range(tm // LANE):
            ls = slice(lc * LANE, (lc + 1) * LANE)
            a1, n1, b2, r2 = _route_head(s1[:, ls], s2[:, ls])
            a1_ref[h, :, ls] = a1
            n1_ref[h, :, ls] = n1
            b2_ref[h, :, ls] = b2.astype(BF16)
            r2_ref[h, :, ls] = r2.astype(BF16)
        return carry
    lax.fori_loop(0, PEER_HEADS, head, 0)


def _router(x, g, wq, keys, tm=512):
    t, d = x.shape
    tab = pl.BlockSpec((PEER_HEADS, PEER_NKEYS, tm), lambda i: (0, 0, i))
    tab_shape = (PEER_HEADS, PEER_NKEYS, t)
    return pl.pallas_call(
        functools.partial(_router_kernel, tm=tm), grid=(t // tm,),
        in_specs=[pl.BlockSpec((tm, d), lambda i: (i, 0)), _resident((1, d)), _resident(wq.shape),
                  _resident(keys.shape)],
        out_specs=[pl.BlockSpec((tm, d), lambda i: (i, 0)), tab, tab, tab, tab],
        out_shape=[jax.ShapeDtypeStruct((t, d), BF16)]
        + [jax.ShapeDtypeStruct(tab_shape, dt) for dt in (F32, F32, BF16, BF16)],
        scratch_shapes=[pltpu.VMEM((2 * PEER_HEADS, tm, PEER_DKEY // 2), BF16)],
        compiler_params=_cparams(("parallel",)), name="peer_router",
    )(x, g, wq, keys)


def _gelu(x):
    return 0.5 * x * (1.0 + lax.erf(x * (2.0 ** -0.5)))


def _peer_kernel(x_ref, hn_ref, a1_ref, n1_ref, b2_ref, r2_ref, u_ref, vt_ref, y_ref, acc_ref, w_ref,
                 *, eb):
    e = pl.program_id(1)
    tb = hn_ref.shape[0]

    @pl.when(e == 0)
    def _():
        acc_ref[...] = jnp.zeros_like(acc_ref)

    tiles = PEER_NKEYS // BF16_ROWS
    slabs = eb // PEER_NKEYS
    per = slabs // PEER_KEY_CHUNKS

    def key_chunk(c):
        return _dot_nt(u_ref[c * per * PEER_NKEYS:(c + 1) * per * PEER_NKEYS, :], hn_ref[...])

    def gate(il):
        g = None
        for h in range(PEER_HEADS):
            n_i = jnp.broadcast_to(n1_ref[h, il:il + 1, :], (BF16_ROWS, tb)).astype(BF16)[None]
            a_i = jnp.broadcast_to(a1_ref[h, il:il + 1, :], (BF16_ROWS, tb)).astype(BF16)[None]
            term = jnp.where(r2_ref[h] < n_i, b2_ref[h], 0.0) * a_i
            g = term if g is None else g + term
        return g

    gates = {il: gate(il) for il in range(per)}
    ahead = key_chunk(0)
    for c in range(PEER_KEY_CHUNKS):
        pre = ahead
        if c + 1 < PEER_KEY_CHUNKS:
            ahead = key_chunk(c + 1)
            gates.update({il: gate(il) for il in range((c + 1) * per, (c + 2) * per)})
        for r in range(per):
            il = c * per + r
            act = _gelu(pre[r * PEER_NKEYS:(r + 1) * PEER_NKEYS, :]).astype(BF16)
            w_ref[il * tiles:(il + 1) * tiles] = act.reshape(tiles, BF16_ROWS, tb) * gates.pop(il)
    acc_ref[...] += jnp.dot(vt_ref[0], w_ref[...].reshape(eb, tb), preferred_element_type=F32)

    @pl.when(e == pl.num_programs(1) - 1)
    def _():
        y_ref[...] = x_ref[...] + acc_ref[...].T


def _peer(x, hn, a1, n1, b2, r2, u, vt, tb=512):
    t, d = hn.shape
    nblk, _, eb = vt.shape
    tiles = PEER_NKEYS // BF16_ROWS
    once = dict(pipeline_mode=pl.Buffered(1))
    key1 = pl.BlockSpec((PEER_HEADS, eb // PEER_NKEYS, tb), lambda i, e: (0, e, i))
    key2 = pl.BlockSpec((PEER_HEADS, tiles, BF16_ROWS, tb), lambda i, e: (0, 0, 0, i), **once)
    return pl.pallas_call(
        functools.partial(_peer_kernel, eb=eb), grid=(t // tb, nblk),
        in_specs=[pl.BlockSpec((tb, d), lambda i, e: (i, 0), **once),
                  pl.BlockSpec((tb, d), lambda i, e: (i, 0), **once), key1, key1, key2, key2,
                  pl.BlockSpec((eb, d), lambda i, e: (e, 0)),
                  pl.BlockSpec((1, d, eb), lambda i, e: (e, 0, 0))],
        out_specs=pl.BlockSpec((tb, d), lambda i, e: (i, 0)),
        out_shape=jax.ShapeDtypeStruct((t, d), F32),
        scratch_shapes=[pltpu.VMEM((d, tb), F32),
                        pltpu.VMEM((eb // BF16_ROWS, BF16_ROWS, tb), BF16)],
        compiler_params=_cparams(("parallel", "arbitrary")), name="peer_experts",
    )(x, hn, a1, n1, b2, r2, u, vt)


def _t5_buckets(rel):
    nb = REL_BUCKETS // 2
    max_exact = nb // 2
    n = np.abs(rel)
    large = max_exact + (np.log(np.maximum(n, 1) / max_exact) / np.log(REL_MAX_DIST / max_exact)
                         * (nb - max_exact)).astype(np.int64)
    large = np.minimum(large, nb - 1)
    return np.where(rel > 0, nb, 0) + np.where(n < max_exact, n, large)


def _lookup(table, idx, n):
    onehot = jnp.asarray(np.arange(n)[:, None] == np.asarray(idx)[None, :], F32)
    return jnp.einsum("...d,dn->...n", table, onehot, precision=lax.Precision.HIGHEST)


def _window_bias(rel_bias):
    span = A_BLOCK + 2 * A_WINDOW
    rel = (np.arange(span)[None, :] - A_WINDOW) - np.arange(A_BLOCK)[:, None]
    bias = _lookup(rel_bias.astype(F32).T, _t5_buckets(rel).reshape(-1), REL_BUCKETS)
    bias = bias.reshape(A_HEADS, A_BLOCK, span)
    bias = jnp.where(jnp.asarray(np.abs(rel) <= A_WINDOW)[None], bias, NEG)
    return bias.reshape(A_KV_HEADS, (A_HEADS // A_KV_HEADS) * A_BLOCK, span)


def _neighbourhood_patterns(rows):
    kh = min(NA_MAX_KH, rows)
    span = kh + B_QROWS
    assert rows % B_QROWS == 0 and rows >= span
    m = np.arange(rows // B_QROWS)[:, None, None]
    q_row = m * B_QROWS + np.arange(B_QROWS)[None, :, None]
    k_row = np.clip(m * B_QROWS - kh // 2, 0, rows - span) + np.arange(span)[None, None, :]
    r0 = np.clip(q_row - kh // 2, 0, rows - kh)
    row_ok = (k_row >= r0) & (k_row < r0 + kh)
    dr = np.where(row_ok, k_row - q_row + NA_MAX_KH - 1, 0)
    assert dr.min() >= 0 and dr.max() <= 2 * NA_MAX_KH - 2
    key = np.concatenate([dr.reshape(len(m), -1), row_ok.reshape(len(m), -1)], axis=1)
    _, first_idx, ids = np.unique(key, axis=0, return_index=True, return_inverse=True)
    return ids.reshape(-1).astype(np.int32), dr[first_idx], row_ok[first_idx]


def _neighbourhood_bias(rpb, rows):
    ids, dr, row_ok = _neighbourhood_patterns(rows)
    qc = np.arange(GRID_W)[:, None]
    kc = np.arange(GRID_W)[None, :]
    c_start = np.clip(qc - NA_KW // 2, 0, GRID_W - NA_KW)
    col_ok = (kc >= c_start) & (kc < c_start + NA_KW)
    dc = np.clip(kc - qc + NA_KW - 1, 0, 2 * NA_KW - 2)
    tab = _lookup(rpb.astype(F32)[:, dr], dc.reshape(-1), 2 * NA_KW - 1)
    tab = tab.reshape(tab.shape[:-1] + dc.shape)
    ok = row_ok[:, :, :, None, None] & col_ok[None, None, None]
    tab = jnp.where(jnp.asarray(ok)[None], tab, NEG)
    tab = jnp.transpose(tab, (0, 1, 2, 4, 3, 5))
    npat, nqr, nkr = dr.shape
    return tab.reshape(B_HEADS, npat, nqr * GRID_W, nkr * GRID_W), jnp.asarray(ids)


def _rope_tables(seq):
    half = HEAD_DIM // 2
    nf = half // 2
    inv = (ROPE_THETA ** (-np.arange(nf) * 2.0 / half)).astype(np.float32).astype(np.float64)
    t = np.arange(seq)
    pos = np.stack([t // GRID_W, t % GRID_W], axis=1).astype(np.float64)
    ang = pos[:, :, None] * inv[None, None, :]
    cos = np.repeat(np.cos(ang), 2, axis=1).reshape(seq, 2, 2, nf)
    sin = np.repeat(np.sin(ang), 2, axis=1).reshape(seq, 2, 2, nf)
    sin = sin * np.array([-1.0, 1.0])[None, None, :, None]
    cos = np.tile(cos.reshape(seq, HEAD_DIM), (1, CHUNK // HEAD_DIM))
    sin = np.tile(sin.reshape(seq, HEAD_DIM), (1, CHUNK // HEAD_DIM))
    return jnp.asarray(cos, F32), jnp.asarray(sin, F32)


def _permute_cols(w):
    return jnp.concatenate([w[..., _REF_SECTIONS[n][0]:_REF_SECTIONS[n][0] + _REF_SECTIONS[n][1]]
                            for n in _PERM_ORDER], axis=-1)


def _proj_gains(ga, gb, gc):
    scale = HEAD_DIM ** -0.5
    one = jnp.ones((HEAD_DIM,), F32)
    per = dict(qa=(ga[0] * scale, A_HEADS), ka=(ga[1], A_KV_HEADS), va=(one, A_KV_HEADS),
               qb=(gb[0] * scale, B_HEADS), kb=(gb[1], B_HEADS), vb=(one, B_HEADS),
               qc=(gc[0] * scale, C_HEADS), kc=(gc[1], C_KV_HEADS), vc=(one, C_KV_HEADS))
    return jnp.concatenate([jnp.tile(per[n][0].astype(F32), per[n][1]) for n in _PERM_ORDER])[None]


def kernel(x, mem, t5_rel_bias, norm_mix, w_in, qk_norm_a, sink_a, qk_norm_b, rpb_b, qk_norm_c,
           out_norm, w_out, norm_mem, norm_mem_kv, w_mem_q, w_mem_kv, qk_norm_mem, w_mem_o,
           norm_ffn, peer_w_q, peer_keys, peer_u, peer_v):
    b, s, d = x.shape
    depth = w_in.shape[0]
    t = b * s
    cos, sin = _rope_tables(s)
    blockdiag = jnp.asarray(np.kron(np.eye(CHUNK // HEAD_DIM), np.ones((HEAD_DIM, HEAD_DIM))), BF16)
    bias_a = _window_bias(t5_rel_bias)
    xf = x.reshape(t, d)
    for l in range(depth):
        qkv = _inproj(xf, norm_mix[l][None], _permute_cols(w_in[l].astype(BF16)),
                      _proj_gains(qk_norm_a[l], qk_norm_b[l], qk_norm_c[l]), cos, sin,
                      blockdiag, s).reshape(b, s, D_IN)
        g_out = out_norm[l]
        ma = _attn_a(qkv, sink_a[l].astype(F32), bias_a, g_out[None, :A_Q])
        bias_b, ids_b = _neighbourhood_bias(rpb_b[l], s // GRID_W)
        mb = _attn_b(qkv, bias_b, ids_b, g_out[None, A_Q:A_Q + B_W])
        mc = _attn_c(qkv, g_out[None, A_Q + B_W:])
        xf = _outproj(xf, ma.reshape(t, A_Q), mb.reshape(t, B_W), mc.reshape(t, C_Q),
                      w_out[l].astype(BF16))
        km, vm = _memkv(mem, norm_mem_kv[l][None], w_mem_kv[l].astype(BF16), qk_norm_mem[l, 1][None])
        xf = _memattn(xf.reshape(b, s, d), norm_mem[l][None], w_mem_q[l].astype(BF16),
                      qk_norm_mem[l, 0][None], km, vm, w_mem_o[l].astype(BF16)).reshape(t, d)
        keys = peer_keys[l].reshape(2 * PEER_HEADS, PEER_NKEYS, PEER_DKEY // 2).astype(BF16)
        hn, a1, n1, b2, r2 = _router(xf, norm_ffn[l][None], peer_w_q[l].astype(BF16), keys)
        fold = (PEER_HEADS, PEER_NKEYS // BF16_ROWS, BF16_ROWS, t)
        xf = _peer(xf, hn, a1, n1, b2.reshape(fold), r2.reshape(fold),
                   peer_u[l].astype(BF16),
                   peer_v[l].astype(BF16).reshape(-1, PEER_EB, d).transpose(0, 2, 1))
    return xf.reshape(b, s, d)
```

```python
import functools

import numpy as np
import jax
import jax.numpy as jnp
from jax import lax
from jax.experimental import pallas as pl
from jax.experimental.pallas import tpu as pltpu

F32 = jnp.float32
BF16 = jnp.bfloat16

GRID_W = 64
HEAD_DIM = 64
EPS = 1e-6
NEG = -1e30
A_HEADS, A_KV_HEADS, A_WINDOW, A_BLOCK = 12, 4, 128, 128
B_HEADS, NA_MAX_KH, NA_KW = 8, 8, 16
A_QBLOCKS = 4
B_QROWS = 4
C_HEADS, C_KV_HEADS = 12, 4
C_BLOCK = 256
ROPE_THETA = 10000.0
REL_BUCKETS, REL_MAX_DIST = 32, 128
MEM_HEADS, MEM_HEAD_DIM = 4, 128
MEM_W = MEM_HEADS * MEM_HEAD_DIM
PEER_HEADS, PEER_NKEYS, PEER_DKEY, PEER_TOPK = 8, 128, 256, 16
A_Q, A_KV = A_HEADS * HEAD_DIM, A_KV_HEADS * HEAD_DIM
B_W = B_HEADS * HEAD_DIM
C_Q, C_KV = C_HEADS * HEAD_DIM, C_KV_HEADS * HEAD_DIM
D_MIX = A_Q + B_W + C_Q
D_IN = A_Q + 2 * A_KV + 3 * B_W + C_Q + 2 * C_KV

LANE = 128
BF16_ROWS = 16
PEER_EB = 1024
PEER_KEY_CHUNKS = 4
CHUNK = 256
VMEM_LIMIT = 56 * 1024 * 1024

_REF_SECTIONS = dict(qa=(0, A_Q), ka=(A_Q, A_KV), va=(A_Q + A_KV, A_KV),
                     qb=(A_Q + 2 * A_KV, B_W), kb=(A_Q + 2 * A_KV + B_W, B_W),
                     vb=(A_Q + 2 * A_KV + 2 * B_W, B_W),
                     qc=(A_Q + 2 * A_KV + 3 * B_W, C_Q),
                     kc=(A_Q + 2 * A_KV + 3 * B_W + C_Q, C_KV),
                     vc=(A_Q + 2 * A_KV + 3 * B_W + C_Q + C_KV, C_KV))
_PERM_ORDER = ("qa", "qc", "qb", "kb", "vb", "ka", "va", "kc", "vc")
_KIND = dict(qa="norm", qc="rope", qb="norm", kb="norm", vb="plain", ka="norm", va="plain",
             kc="rope", vc="plain")
_PERM_OFF = {}
_off = 0
for _n in _PERM_ORDER:
    _PERM_OFF[_n] = _off
    _off += _REF_SECTIONS[_n][1]
_CHUNK_KINDS = []
for _n in _PERM_ORDER:
    _CHUNK_KINDS += [_KIND[_n]] * (_REF_SECTIONS[_n][1] // CHUNK)


def _cparams(sem):
    return pltpu.CompilerParams(dimension_semantics=sem, vmem_limit_bytes=VMEM_LIMIT)


def _resident(shape):
    nd = len(shape)
    return pl.BlockSpec(shape, lambda *_: (0,) * nd, pipeline_mode=pl.Buffered(1))


def _rms(x, g):
    return x * lax.rsqrt(jnp.mean(x * x, axis=-1, keepdims=True) + EPS) * g


def _pv(p, v):
    half = (p.shape[1] // 2) // LANE * LANE
    return (jnp.dot(p[:, :half], v[:half], preferred_element_type=F32)
            + jnp.dot(p[:, half:], v[half:], preferred_element_type=F32))


def _dot_nt(a, b):
    return lax.dot_general(a, b, (((1,), (1,)), ((), ())), preferred_element_type=F32)


def _inproj_kernel(x_ref, g_ref, w_ref, gain_ref, cos_ref, sin_ref, bd_ref, qkv_ref):
    hn = _rms(x_ref[...], g_ref[...]).astype(BF16)
    lane = lax.broadcasted_iota(jnp.int32, (1, CHUNK), 1)
    lower_half = (lane % 32) < 16
    for c, kind in enumerate(_CHUNK_KINDS):
        cols = slice(c * CHUNK, (c + 1) * CHUNK)
        if c % 2 == 0:
            pair = jnp.dot(hn, w_ref[:, c * CHUNK:(c + 2) * CHUNK], preferred_element_type=F32)
        acc = pair[:, (c % 2) * CHUNK:(c % 2 + 1) * CHUNK]
        if kind != "plain":
            ss = jnp.dot((acc * acc).astype(BF16), bd_ref[...], preferred_element_type=F32)
            acc = acc * lax.rsqrt(ss * (1.0 / HEAD_DIM) + EPS) * gain_ref[:, cols]
        if kind == "rope":
            partner = jnp.where(lower_half, pltpu.roll(acc, CHUNK - 16, 1), pltpu.roll(acc, 16, 1))
            acc = acc * cos_ref[...] + partner * sin_ref[...]
        qkv_ref[:, cols] = acc.astype(BF16)


def _inproj(x, g, w, gain, cos, sin, bd, seq, tm=512):
    t, d = x.shape
    n = w.shape[1]
    nseq = seq // tm
    tab = pl.BlockSpec((tm, CHUNK), lambda i: (i % nseq, 0))
    return pl.pallas_call(
        _inproj_kernel, grid=(t // tm,),
        in_specs=[pl.BlockSpec((tm, d), lambda i: (i, 0)), _resident((1, d)), _resident((d, n)),
                  _resident((1, n)), tab, tab, _resident((CHUNK, CHUNK))],
        out_specs=pl.BlockSpec((tm, n), lambda i: (i, 0)),
        out_shape=jax.ShapeDtypeStruct((t, n), BF16), compiler_params=_cparams(("parallel",)),
        name="inproj")(x, g, w, gain, cos, sin, bd)


def _attn_a_kernel(sink_ref, q_ref, k_ref, v_ref, bias_ref, gout_ref, o_ref):
    nb = k_ref.shape[1] // A_BLOCK
    for sub in range(A_QBLOCKS):
        j = pl.program_id(1) * A_QBLOCKS + sub
        rows = slice(sub * A_BLOCK, (sub + 1) * A_BLOCK)
        o_ref[0, rows, :] = _window_block(sink_ref, q_ref[0, rows, :], k_ref, v_ref, bias_ref,
                                          gout_ref, j, nb).astype(o_ref.dtype)


def _window_block(sink_ref, q, k_ref, v_ref, bias_ref, gout_ref, j, nb):
    def three(ref):
        return jnp.concatenate(
            [ref[0, pl.ds(pl.multiple_of(jnp.clip(j + d, 0, nb - 1) * A_BLOCK, A_BLOCK), A_BLOCK), :]
             for d in (-1, 0, 1)], axis=0)
    k = three(k_ref)
    v = three(v_ref)
    col = lax.broadcasted_iota(jnp.int32, (1, 3 * A_BLOCK), 1)
    first_ok = jnp.where(j > 0, 0, A_BLOCK)
    end_ok = jnp.where(j < nb - 1, 3 * A_BLOCK, 2 * A_BLOCK)
    ok = (col >= first_ok) & (col < end_ok)
    g = A_HEADS // A_KV_HEADS
    row = lax.broadcasted_iota(jnp.int32, (g * A_BLOCK, 1), 0)
    outs = []
    for hk in range(A_KV_HEADS):
        kh = k[:, hk * HEAD_DIM:(hk + 1) * HEAD_DIM]
        vh = v[:, hk * HEAD_DIM:(hk + 1) * HEAD_DIM]
        qs = jnp.concatenate([q[:, (hk * g + gi) * HEAD_DIM:(hk * g + gi + 1) * HEAD_DIM]
                              for gi in range(g)], axis=0)
        s = jnp.full((g * A_BLOCK, 1), sink_ref[hk * g], F32)
        for gi in range(1, g):
            s = jnp.where(row >= gi * A_BLOCK, sink_ref[hk * g + gi], s)
        logits = jnp.where(ok, _dot_nt(qs, kh) + bias_ref[hk], NEG)
        m = jnp.maximum(jnp.max(logits, axis=-1, keepdims=True), s)
        p = jnp.exp(logits - m)
        denom = jnp.sum(p, axis=-1, keepdims=True) + jnp.exp(s - m)
        o = _pv(p.astype(BF16), vh) / denom
        outs += [o[gi * A_BLOCK:(gi + 1) * A_BLOCK] for gi in range(g)]
    return _rms(jnp.concatenate(outs, axis=-1), gout_ref[...])


def _attn_a(qkv, sink, bias, gout):
    b, s, _ = qkv.shape
    tq = A_QBLOCKS * A_BLOCK
    return pl.pallas_call(
        _attn_a_kernel, grid=(b, s // tq),
        in_specs=[pl.BlockSpec(memory_space=pltpu.SMEM),
                  pl.BlockSpec((1, tq, A_Q), lambda bi, j: (bi, j, _PERM_OFF["qa"] // A_Q)),
                  pl.BlockSpec((1, s, A_KV), lambda bi, j: (bi, 0, _PERM_OFF["ka"] // A_KV)),
                  pl.BlockSpec((1, s, A_KV), lambda bi, j: (bi, 0, _PERM_OFF["va"] // A_KV)),
                  _resident(bias.shape), _resident((1, A_Q))],
        out_specs=pl.BlockSpec((1, tq, A_Q), lambda bi, j: (bi, j, 0)),
        out_shape=jax.ShapeDtypeStruct((b, s, A_Q), BF16),
        compiler_params=_cparams(("parallel", "arbitrary")), name="attn_a",
    )(sink, qkv, qkv, qkv, bias, gout)


def _attn_b_kernel(ids_ref, q_ref, k_ref, v_ref, bias_ref, gout_ref, o_ref, *, rows, kh):
    del ids_ref
    m = pl.program_id(1)
    span = kh + B_QROWS
    first = jnp.clip(m * B_QROWS - kh // 2, 0, rows - span)
    start = pl.multiple_of(first * GRID_W, GRID_W)
    q = q_ref[0]
    k = k_ref[0, pl.ds(start, span * GRID_W), :]
    v = v_ref[0, pl.ds(start, span * GRID_W), :]
    outs = []
    for h in range(B_HEADS):
        hs = slice(h * HEAD_DIM, (h + 1) * HEAD_DIM)
        logits = _dot_nt(q[:, hs], k[:, hs]) + bias_ref[h, 0]
        m = jnp.max(logits, axis=-1, keepdims=True)
        p = jnp.exp(logits - m)
        denom = jnp.sum(p, axis=-1, keepdims=True)
        outs.append(_pv(p.astype(BF16), v[:, hs]) / denom)
    o = jnp.concatenate(outs, axis=-1)
    o_ref[0] = _rms(o, gout_ref[...]).astype(o_ref.dtype)


def _attn_b(qkv, bias, bias_ids, gout):
    b, s, _ = qkv.shape
    rows = s // GRID_W
    kh = min(NA_MAX_KH, rows)
    nq, nk = B_QROWS * GRID_W, (kh + B_QROWS) * GRID_W
    grid_spec = pltpu.PrefetchScalarGridSpec(
        num_scalar_prefetch=1, grid=(b, rows // B_QROWS),
        in_specs=[pl.BlockSpec((1, nq, B_W), lambda bi, m, ids: (bi, m, _PERM_OFF["qb"] // B_W)),
                  pl.BlockSpec((1, s, B_W), lambda bi, m, ids: (bi, 0, _PERM_OFF["kb"] // B_W)),
                  pl.BlockSpec((1, s, B_W), lambda bi, m, ids: (bi, 0, _PERM_OFF["vb"] // B_W)),
                  pl.BlockSpec((B_HEADS, 1, nq, nk), lambda bi, m, ids: (0, ids[m], 0, 0)),
                  pl.BlockSpec((1, B_W), lambda bi, m, ids: (0, 0))],
        out_specs=pl.BlockSpec((1, nq, B_W), lambda bi, m, ids: (bi, m, 0)))
    return pl.pallas_call(
        functools.partial(_attn_b_kernel, rows=rows, kh=kh), grid_spec=grid_spec,
        out_shape=jax.ShapeDtypeStruct((b, s, B_W), BF16),
        compiler_params=_cparams(("parallel", "arbitrary")), name="attn_b",
    )(bias_ids, qkv, qkv, qkv, bias, gout)


def _attn_c_kernel(q_ref, k_ref, v_ref, gout_ref, o_ref):
    q = q_ref[0]
    k = k_ref[0]
    v = v_ref[0]
    g = C_HEADS // C_KV_HEADS

    def group_logits(hk):
        kh = k[:, hk * HEAD_DIM:(hk + 1) * HEAD_DIM]
        qs = jnp.concatenate([q[:, (hk * g + gi) * HEAD_DIM:(hk * g + gi + 1) * HEAD_DIM]
                              for gi in range(g)], axis=0)
        return _dot_nt(qs, kh)

    outs = []
    ahead = group_logits(0)
    for hk in range(C_KV_HEADS):
        vh = v[:, hk * HEAD_DIM:(hk + 1) * HEAD_DIM]
        logits = ahead
        if hk + 1 < C_KV_HEADS:
            ahead = group_logits(hk + 1)
        m = jnp.max(logits, axis=-1, keepdims=True)
        p = jnp.exp(logits - m)
        denom = jnp.sum(p, axis=-1, keepdims=True)
        o = _pv(p.astype(BF16), vh) / denom
        outs += [o[gi * C_BLOCK:(gi + 1) * C_BLOCK] for gi in range(g)]
    o = jnp.concatenate(outs, axis=-1)
    o_ref[0] = _rms(o, gout_ref[...]).astype(o_ref.dtype)


def _attn_c(qkv, gout):
    b, s, _ = qkv.shape
    return pl.pallas_call(
        _attn_c_kernel, grid=(b, s // C_BLOCK),
        in_specs=[pl.BlockSpec((1, C_BLOCK, C_Q), lambda bi, j: (bi, j, _PERM_OFF["qc"] // C_Q)),
                  pl.BlockSpec((1, s, C_KV), lambda bi, j: (bi, 0, _PERM_OFF["kc"] // C_KV)),
                  pl.BlockSpec((1, s, C_KV), lambda bi, j: (bi, 0, _PERM_OFF["vc"] // C_KV)),
                  _resident((1, C_Q))],
        out_specs=pl.BlockSpec((1, C_BLOCK, C_Q), lambda bi, j: (bi, j, 0)),
        out_shape=jax.ShapeDtypeStruct((b, s, C_Q), BF16),
        compiler_params=_cparams(("parallel", "arbitrary")), name="attn_c",
    )(qkv, qkv, qkv, gout)


def _outproj_kernel(x_ref, a_ref, b_ref, c_ref, w_ref, o_ref):
    acc = jnp.dot(a_ref[...], w_ref[0:A_Q, :], preferred_element_type=F32)
    acc += jnp.dot(b_ref[...], w_ref[A_Q:A_Q + B_W, :], preferred_element_type=F32)
    acc += jnp.dot(c_ref[...], w_ref[A_Q + B_W:D_MIX, :], preferred_element_type=F32)
    o_ref[...] = x_ref[...] + acc


def _outproj(x, ma, mb, mc, w, tm=512):
    t, d = x.shape

    def row(n):
        return pl.BlockSpec((tm, n), lambda i: (i, 0))
    return pl.pallas_call(
        _outproj_kernel, grid=(t // tm,),
        in_specs=[row(d), row(A_Q), row(B_W), row(C_Q), _resident(w.shape)],
        out_specs=row(d), out_shape=jax.ShapeDtypeStruct((t, d), F32),
        compiler_params=_cparams(("parallel",)), name="outproj",
    )(x, ma, mb, mc, w)


def _memkv_kernel(mem_ref, g_ref, w_ref, gk_ref, k_ref, v_ref):
    mn = _rms(mem_ref[0], g_ref[...]).astype(BF16)
    kv = jnp.dot(mn, w_ref[...], preferred_element_type=F32)
    for h in range(MEM_HEADS):
        hs = slice(h * MEM_HEAD_DIM, (h + 1) * MEM_HEAD_DIM)
        k_ref[0, :, hs] = _rms(kv[:, hs], gk_ref[...]).astype(BF16)
    v_ref[0] = kv[:, MEM_W:].astype(BF16)


def _memkv(mem, g, w, gk):
    b, m, d = mem.shape
    out = pl.BlockSpec((1, m, MEM_W), lambda i: (i, 0, 0))
    return pl.pallas_call(
        _memkv_kernel, grid=(b,),
        in_specs=[pl.BlockSpec((1, m, d), lambda i: (i, 0, 0)), _resident((1, d)),
                  _resident(w.shape), _resident((1, MEM_HEAD_DIM))],
        out_specs=[out, out], out_shape=[jax.ShapeDtypeStruct((b, m, MEM_W), BF16)] * 2,
        compiler_params=_cparams(("parallel",)), name="memkv",
    )(mem, g, w, gk)


def _memattn_kernel(x_ref, g_ref, wq_ref, gq_ref, k_ref, v_ref, wo_ref, o_ref):
    x = x_ref[0]
    hn = _rms(x, g_ref[...]).astype(BF16)
    q = jnp.dot(hn, wq_ref[...], preferred_element_type=F32)
    outs = []
    for h in range(MEM_HEADS):
        hs = slice(h * MEM_HEAD_DIM, (h + 1) * MEM_HEAD_DIM)
        qh = _rms(q[:, hs], gq_ref[...]).astype(BF16)
        logits = _dot_nt(qh, k_ref[0, :, hs]) * (MEM_HEAD_DIM ** -0.5)
        m = jnp.max(logits, axis=-1, keepdims=True)
        p = jnp.exp(logits - m)
        denom = jnp.sum(p, axis=-1, keepdims=True)
        outs.append(_pv(p.astype(BF16), v_ref[0, :, hs]) / denom)
    o = jnp.concatenate(outs, axis=-1).astype(BF16)
    o_ref[0] = x + jnp.dot(o, wo_ref[...], preferred_element_type=F32)


def _memattn(x, g, wq, gq, km, vm, wo, tm=512):
    b, s, d = x.shape
    m = km.shape[1]
    row = pl.BlockSpec((1, tm, d), lambda bi, i: (bi, i, 0))
    kv = pl.BlockSpec((1, m, MEM_W), lambda bi, i: (bi, 0, 0))
    return pl.pallas_call(
        _memattn_kernel, grid=(b, s // tm),
        in_specs=[row, _resident((1, d)), _resident(wq.shape), _resident((1, MEM_HEAD_DIM)), kv, kv,
                  _resident(wo.shape)],
        out_specs=row, out_shape=jax.ShapeDtypeStruct((b, s, d), F32),
        compiler_params=_cparams(("parallel", "arbitrary")), name="memattn",
    )(x, g, wq, gq, km, vm, wo)


def _sorting_network(n):
    def merge(lo, hi, r):
        step = r * 2
        if step < hi - lo:
            yield from merge(lo, hi, step)
            yield from merge(lo + r, hi, step)
            yield from ((i, i + r) for i in range(lo + r, hi - r, step))
        else:
            yield (lo, lo + r)

    def sort(lo, hi):
        if hi - lo >= 1:
            mid = lo + (hi - lo) // 2
            yield from sort(lo, mid)
            yield from sort(mid + 1, hi)
            yield from merge(lo, hi, 1)
    return tuple(sort(0, n - 1))


_SORT16 = _sorting_network(PEER_NKEYS // 8)


def _top_values(s):
    depth = PEER_NKEYS // 8
    v = [s[8 * i:8 * (i + 1)] for i in range(depth)]
    for i, j in _SORT16:
        v[i], v[j] = jnp.maximum(v[i], v[j]), jnp.minimum(v[i], v[j])
    tops = []
    for kk in range(PEER_TOPK):
        mk = jnp.max(v[0], axis=0, keepdims=True)
        tops.append(mk)
        pop = v[0] == mk
        for i in range(PEER_TOPK - 1 - kk):
            v[i] = jnp.where(pop, v[i + 1], v[i])
    return tops


def _rank_among(tops, s):
    rank = jnp.full(s.shape, float(PEER_TOPK), F32)
    for kk in reversed(range(PEER_TOPK)):
        rank = jnp.where(s >= tops[kk], float(kk), rank)
    return rank


def _stack8(rows):
    sub = lax.broadcasted_iota(jnp.int32, (8, rows[0].shape[1]), 0)
    out = jnp.broadcast_to(rows[0], sub.shape)
    for kk in range(1, 8):
        out = jnp.where(sub == kk, rows[kk], out)
    return out


def _route_head(s1, s2):
    t1 = _top_values(s1)
    t2 = _top_values(s2)
    r2 = _rank_among(t2, s2)
    ts2_lo = _stack8(t2[:8])
    ts2_hi = _stack8(t2[8:])
    sub = lax.broadcasted_iota(jnp.int32, ts2_lo.shape, 0)
    pieces = [t1[0] + ts2_lo, t1[0] + ts2_hi]
    for k1 in range(1, 8):
        pieces.append(jnp.where(sub < PEER_TOPK // (k1 + 1), t1[k1] + ts2_lo, -jnp.inf))
    pieces.append(_stack8(t1[8:]) + t2[0])
    cand = jnp.concatenate(pieces, axis=0)
    best = []
    for _ in range(PEER_TOPK):
        mk = jnp.max(cand, axis=0, keepdims=True)
        cand = jnp.where(cand == mk, -jnp.inf, cand)
        best.append(mk)
    tau = best[-1]
    top = t1[0] + t2[0]
    z = best[0] - top
    z = jnp.exp(z)
    for bk in best[1:]:
        z = z + jnp.exp(bk - top)
    n1 = jnp.zeros(s1.shape, F32)
    for k2 in range(PEER_TOPK // 2):
        n1 = n1 + jnp.where(s1 + t2[k2] >= tau, 1.0, 0.0)
    deep = jnp.zeros(tau.shape, F32)
    for k2 in range(PEER_TOPK // 2, PEER_TOPK):
        deep = deep + jnp.where(t1[0] + t2[k2] >= tau, 1.0, 0.0)
    n1 = n1 + jnp.where(s1 == t1[0], deep, 0.0)
    a1 = jnp.exp(s1 - t1[0]) / z
    b2 = jnp.exp(s2 - t2[0])
    return a1, n1, b2, r2


def _router_kernel(x_ref, g_ref, wq_ref, keys_ref, hn_ref, a1_ref, n1_ref, b2_ref, r2_ref, q_scr,
                   *, tm):
    hn = _rms(x_ref[...], g_ref[...]).astype(BF16)
    hn_ref[...] = hn
    half = PEER_DKEY // 2
    wide = wq_ref.shape[1] // 2
    for part in range(2):
        q = jnp.dot(hn, wq_ref[:, part * wide:(part + 1) * wide], preferred_element_type=F32)
        for c in range(wide // half):
            q_scr[part * (wide // half) + c] = q[:, c * half:(c + 1) * half].astype(BF16)

    def head(h, carry):
        s1 = _dot_nt(keys_ref[2 * h], q_scr[2 * h])
        s2 = _dot_nt(keys_ref[2 * h + 1], q_scr[2 * h + 1])
        for lc in range(tm // LANE):
            ls = slice(lc * LANE, (lc + 1) * LANE)
            a1, n1, b2, r2 = _route_head(s1[:, ls], s2[:, ls])
            a1_ref[h, :, ls] = a1
            n1_ref[h, :, ls] = n1
            fold = (PEER_NKEYS // BF16_ROWS, BF16_ROWS, LANE)
            b2_ref[h, :, :, ls] = b2.astype(BF16).reshape(fold)
            r2_ref[h, :, :, ls] = r2.astype(BF16).reshape(fold)
        return carry
    lax.fori_loop(0, PEER_HEADS, head, 0)


def _router(x, g, wq, keys, tm=512):
    t, d = x.shape
    tiles = PEER_NKEYS // BF16_ROWS
    tab = pl.BlockSpec((PEER_HEADS, PEER_NKEYS, tm), lambda i: (0, 0, i))
    tab16 = pl.BlockSpec((PEER_HEADS, tiles, BF16_ROWS, tm), lambda i: (0, 0, 0, i))
    return pl.pallas_call(
        functools.partial(_router_kernel, tm=tm), grid=(t // tm,),
        in_specs=[pl.BlockSpec((tm, d), lambda i: (i, 0)), _resident((1, d)), _resident(wq.shape),
                  _resident(keys.shape)],
        out_specs=[pl.BlockSpec((tm, d), lambda i: (i, 0)), tab, tab, tab16, tab16],
        out_shape=[jax.ShapeDtypeStruct((t, d), BF16)]
        + [jax.ShapeDtypeStruct((PEER_HEADS, PEER_NKEYS, t), F32)] * 2
        + [jax.ShapeDtypeStruct((PEER_HEADS, tiles, BF16_ROWS, t), BF16)] * 2,
        scratch_shapes=[pltpu.VMEM((2 * PEER_HEADS, tm, PEER_DKEY // 2), BF16)],
        compiler_params=_cparams(("parallel",)), name="peer_router",
    )(x, g, wq, keys)


def _gelu(x):
    return 0.5 * x * (1.0 + lax.erf(x * (2.0 ** -0.5)))


def _peer_kernel(x_ref, hn_ref, a1_ref, n1_ref, b2_ref, r2_ref, u_ref, vt_ref, y_ref, acc_ref, w_ref,
                 *, eb):
    e = pl.program_id(1)
    tb = hn_ref.shape[0]

    @pl.when(e == 0)
    def _():
        acc_ref[...] = jnp.zeros_like(acc_ref)

    tiles = PEER_NKEYS // BF16_ROWS
    slabs = eb // PEER_NKEYS
    per = slabs // PEER_KEY_CHUNKS

    def key_chunk(c):
        return _dot_nt(u_ref[c * per * PEER_NKEYS:(c + 1) * per * PEER_NKEYS, :], hn_ref[...])

    def gate(il):
        g = None
        for h in range(PEER_HEADS):
            n_i = jnp.broadcast_to(n1_ref[h, il:il + 1, :], (BF16_ROWS, tb)).astype(BF16)[None]
            a_i = jnp.broadcast_to(a1_ref[h, il:il + 1, :], (BF16_ROWS, tb)).astype(BF16)[None]
            term = jnp.where(r2_ref[h] < n_i, b2_ref[h], 0.0) * a_i
            g = term if g is None else g + term
        return g

    gates = {il: gate(il) for il in range(per)}
    ahead = key_chunk(0)
    for c in range(PEER_KEY_CHUNKS):
        pre = ahead
        if c + 1 < PEER_KEY_CHUNKS:
            ahead = key_chunk(c + 1)
            gates.update({il: gate(il) for il in range((c + 1) * per, (c + 2) * per)})
        for r in range(per):
            il = c * per + r
            act = _gelu(pre[r * PEER_NKEYS:(r + 1) * PEER_NKEYS, :]).astype(BF16)
            w_ref[il * tiles:(il + 1) * tiles] = act.reshape(tiles, BF16_ROWS, tb) * gates.pop(il)
    acc_ref[...] += jnp.dot(vt_ref[0], w_ref[...].reshape(eb, tb), preferred_element_type=F32)

    @pl.when(e == pl.num_programs(1) - 1)
    def _():
        y_ref[...] = x_ref[...] + acc_ref[...].T


def _peer(x, hn, a1, n1, b2, r2, u, vt, tb=512):
    t, d = hn.shape
    nblk, _, eb = vt.shape
    tiles = PEER_NKEYS // BF16_ROWS
    key1 = pl.BlockSpec((PEER_HEADS, eb // PEER_NKEYS, tb), lambda i, e: (0, e, i))
    key2 = pl.BlockSpec((PEER_HEADS, tiles, BF16_ROWS, tb), lambda i, e: (0, 0, 0, i))
    return pl.pallas_call(
        functools.partial(_peer_kernel, eb=eb), grid=(t // tb, nblk),
        in_specs=[pl.BlockSpec((tb, d), lambda i, e: (i, 0)),
                  pl.BlockSpec((tb, d), lambda i, e: (i, 0)), key1, key1, key2, key2,
                  pl.BlockSpec((eb, d), lambda i, e: (e, 0)),
                  pl.BlockSpec((1, d, eb), lambda i, e: (e, 0, 0))],
        out_specs=pl.BlockSpec((tb, d), lambda i, e: (i, 0)),
        out_shape=jax.ShapeDtypeStruct((t, d), F32),
        scratch_shapes=[pltpu.VMEM((d, tb), F32),
                        pltpu.VMEM((eb // BF16_ROWS, BF16_ROWS, tb), BF16)],
        compiler_params=_cparams(("parallel", "arbitrary")), name="peer_experts",
    )(x, hn, a1, n1, b2, r2, u, vt)


def _t5_buckets(rel):
    nb = REL_BUCKETS // 2
    max_exact = nb // 2
    n = np.abs(rel)
    large = max_exact + (np.log(np.maximum(n, 1) / max_exact) / np.log(REL_MAX_DIST / max_exact)
                         * (nb - max_exact)).astype(np.int64)
    large = np.minimum(large, nb - 1)
    return np.where(rel > 0, nb, 0) + np.where(n < max_exact, n, large)


def _lookup(table, idx, n):
    onehot = jnp.asarray(np.arange(n)[:, None] == np.asarray(idx)[None, :], F32)
    return jnp.einsum("...d,dn->...n", table, onehot, precision=lax.Precision.HIGHEST)


def _window_bias(rel_bias):
    span = A_BLOCK + 2 * A_WINDOW
    rel = (np.arange(span)[None, :] - A_WINDOW) - np.arange(A_BLOCK)[:, None]
    bias = _lookup(rel_bias.astype(F32).T, _t5_buckets(rel).reshape(-1), REL_BUCKETS)
    bias = bias.reshape(A_HEADS, A_BLOCK, span)
    bias = jnp.where(jnp.asarray(np.abs(rel) <= A_WINDOW)[None], bias, NEG)
    return bias.reshape(A_KV_HEADS, (A_HEADS // A_KV_HEADS) * A_BLOCK, span)


def _neighbourhood_patterns(rows):
    kh = min(NA_MAX_KH, rows)
    span = kh + B_QROWS
    assert rows % B_QROWS == 0 and rows >= span
    m = np.arange(rows // B_QROWS)[:, None, None]
    q_row = m * B_QROWS + np.arange(B_QROWS)[None, :, None]
    k_row = np.clip(m * B_QROWS - kh // 2, 0, rows - span) + np.arange(span)[None, None, :]
    r0 = np.clip(q_row - kh // 2, 0, rows - kh)
    row_ok = (k_row >= r0) & (k_row < r0 + kh)
    dr = np.where(row_ok, k_row - q_row + NA_MAX_KH - 1, 0)
    assert dr.min() >= 0 and dr.max() <= 2 * NA_MAX_KH - 2
    key = np.concatenate([dr.reshape(len(m), -1), row_ok.reshape(len(m), -1)], axis=1)
    _, first_idx, ids = np.unique(key, axis=0, return_index=True, return_inverse=True)
    return ids.reshape(-1).astype(np.int32), dr[first_idx], row_ok[first_idx]


def _neighbourhood_bias(rpb, rows):
    ids, dr, row_ok = _neighbourhood_patterns(rows)
    qc = np.arange(GRID_W)[:, None]
    kc = np.arange(GRID_W)[None, :]
    c_start = np.clip(qc - NA_KW // 2, 0, GRID_W - NA_KW)
    col_ok = (kc >= c_start) & (kc < c_start + NA_KW)
    dc = np.clip(kc - qc + NA_KW - 1, 0, 2 * NA_KW - 2)
    tab = _lookup(rpb.astype(F32)[:, dr], dc.reshape(-1), 2 * NA_KW - 1)
    tab = tab.reshape(tab.shape[:-1] + dc.shape)
    ok = row_ok[:, :, :, None, None] & col_ok[None, None, None]
    tab = jnp.where(jnp.asarray(ok)[None], tab, NEG)
    tab = jnp.transpose(tab, (0, 1, 2, 4, 3, 5))
    npat, nqr, nkr = dr.shape
    return tab.reshape(B_HEADS, npat, nqr * GRID_W, nkr * GRID_W), jnp.asarray(ids)


def _rope_tables(seq):
    half = HEAD_DIM // 2
    nf = half // 2
    inv = (ROPE_THETA ** (-np.arange(nf) * 2.0 / half)).astype(np.float32).astype(np.float64)
    t = np.arange(seq)
    pos = np.stack([t // GRID_W, t % GRID_W], axis=1).astype(np.float64)
    ang = pos[:, :, None] * inv[None, None, :]
    cos = np.repeat(np.cos(ang), 2, axis=1).reshape(seq, 2, 2, nf)
    sin = np.repeat(np.sin(ang), 2, axis=1).reshape(seq, 2, 2, nf)
    sin = sin * np.array([-1.0, 1.0])[None, None, :, None]
    cos = np.tile(cos.reshape(seq, HEAD_DIM), (1, CHUNK // HEAD_DIM))
    sin = np.tile(sin.reshape(seq, HEAD_DIM), (1, CHUNK // HEAD_DIM))
    return jnp.asarray(cos, F32), jnp.asarray(sin, F32)


def _permute_cols(w):
    return jnp.concatenate([w[..., _REF_SECTIONS[n][0]:_REF_SECTIONS[n][0] + _REF_SECTIONS[n][1]]
                            for n in _PERM_ORDER], axis=-1)


def _proj_gains(ga, gb, gc):
    scale = HEAD_DIM ** -0.5
    one = jnp.ones((HEAD_DIM,), F32)
    per = dict(qa=(ga[0] * scale, A_HEADS), ka=(ga[1], A_KV_HEADS), va=(one, A_KV_HEADS),
               qb=(gb[0] * scale, B_HEADS), kb=(gb[1], B_HEADS), vb=(one, B_HEADS),
               qc=(gc[0] * scale, C_HEADS), kc=(gc[1], C_KV_HEADS), vc=(one, C_KV_HEADS))
    return jnp.concatenate([jnp.tile(per[n][0].astype(F32), per[n][1]) for n in _PERM_ORDER])[None]


def kernel(x, mem, t5_rel_bias, norm_mix, w_in, qk_norm_a, sink_a, qk_norm_b, rpb_b, qk_norm_c,
           out_norm, w_out, norm_mem, norm_mem_kv, w_mem_q, w_mem_kv, qk_norm_mem, w_mem_o,
           norm_ffn, peer_w_q, peer_keys, peer_u, peer_v):
    b, s, d = x.shape
    depth = w_in.shape[0]
    t = b * s
    cos, sin = _rope_tables(s)
    blockdiag = jnp.asarray(np.kron(np.eye(CHUNK // HEAD_DIM), np.ones((HEAD_DIM, HEAD_DIM))), BF16)
    bias_a = _window_bias(t5_rel_bias)
    xf = x.reshape(t, d)
    for l in range(depth):
        qkv = _inproj(xf, norm_mix[l][None], _permute_cols(w_in[l].astype(BF16)),
                      _proj_gains(qk_norm_a[l], qk_norm_b[l], qk_norm_c[l]), cos, sin,
                      blockdiag, s).reshape(b, s, D_IN)
        g_out = out_norm[l]
        ma = _attn_a(qkv, sink_a[l].astype(F32), bias_a, g_out[None, :A_Q])
        bias_b, ids_b = _neighbourhood_bias(rpb_b[l], s // GRID_W)
        mb = _attn_b(qkv, bias_b, ids_b, g_out[None, A_Q:A_Q + B_W])
        mc = _attn_c(qkv, g_out[None, A_Q + B_W:])
        xf = _outproj(xf, ma.reshape(t, A_Q), mb.reshape(t, B_W), mc.reshape(t, C_Q),
                      w_out[l].astype(BF16))
        km, vm = _memkv(mem, norm_mem_kv[l][None], w_mem_kv[l].astype(BF16), qk_norm_mem[l, 1][None])
        xf = _memattn(xf.reshape(b, s, d), norm_mem[l][None], w_mem_q[l].astype(BF16),
                      qk_norm_mem[l, 0][None], km, vm, w_mem_o[l].astype(BF16)).reshape(t, d)
        keys = peer_keys[l].reshape(2 * PEER_HEADS, PEER_NKEYS, PEER_DKEY // 2).astype(BF16)
        hn, a1, n1, b2, r2 = _router(xf, norm_ffn[l][None], peer_w_q[l].astype(BF16), keys)
        xf = _peer(xf, hn, a1, n1, b2, r2,
                   peer_u[l].astype(BF16),
                   peer_v[l].astype(BF16).reshape(-1, PEER_EB, d).transpose(0, 2, 1))
    return xf.reshape(b, s, d)
```

```python
import functools

import numpy as np
import jax
import jax.numpy as jnp
from jax import lax
from jax.experimental import pallas as pl
from jax.experimental.pallas import tpu as pltpu

F32 = jnp.float32
BF16 = jnp.bfloat16

GRID_W = 64
HEAD_DIM = 64
EPS = 1e-6
NEG = -1e30
A_HEADS, A_KV_HEADS, A_WINDOW, A_BLOCK = 12, 4, 128, 128
B_HEADS, NA_MAX_KH, NA_KW = 8, 8, 16
A_QBLOCKS = 4
B_QROWS = 4
C_HEADS, C_KV_HEADS = 12, 4
C_BLOCK = 256
ROPE_THETA = 10000.0
ROPE_PAIR = HEAD_DIM // 4
REL_BUCKETS, REL_MAX_DIST = 32, 128
MEM_HEADS, MEM_HEAD_DIM = 4, 128
MEM_W = MEM_HEADS * MEM_HEAD_DIM
PEER_HEADS, PEER_NKEYS, PEER_DKEY, PEER_TOPK = 8, 128, 256, 16
A_Q, A_KV = A_HEADS * HEAD_DIM, A_KV_HEADS * HEAD_DIM
B_W = B_HEADS * HEAD_DIM
C_Q, C_KV = C_HEADS * HEAD_DIM, C_KV_HEADS * HEAD_DIM
D_MIX = A_Q + B_W + C_Q
D_IN = A_Q + 2 * A_KV + 3 * B_W + C_Q + 2 * C_KV

LANE = 128
BF16_ROWS = 16
PEER_EB = 1024
PEER_KEY_CHUNKS = 4
CHUNK = 256
VMEM_LIMIT = 56 * 1024 * 1024

_REF_SECTIONS = dict(qa=(0, A_Q), ka=(A_Q, A_KV), va=(A_Q + A_KV, A_KV),
                     qb=(A_Q + 2 * A_KV, B_W), kb=(A_Q + 2 * A_KV + B_W, B_W),
                     vb=(A_Q + 2 * A_KV + 2 * B_W, B_W),
                     qc=(A_Q + 2 * A_KV + 3 * B_W, C_Q),
                     kc=(A_Q + 2 * A_KV + 3 * B_W + C_Q, C_KV),
                     vc=(A_Q + 2 * A_KV + 3 * B_W + C_Q + C_KV, C_KV))
_PERM_ORDER = ("qa", "qc", "qb", "kb", "vb", "ka", "va", "kc", "vc")
_KIND = dict(qa="norm", qc="rope", qb="norm", kb="norm", vb="plain", ka="norm", va="plain",
             kc="rope", vc="plain")
_PERM_OFF = {}
_off = 0
for _n in _PERM_ORDER:
    _PERM_OFF[_n] = _off
    _off += _REF_SECTIONS[_n][1]
_CHUNK_KINDS = []
for _n in _PERM_ORDER:
    _CHUNK_KINDS += [_KIND[_n]] * (_REF_SECTIONS[_n][1] // CHUNK)


def _cparams(sem):
    return pltpu.CompilerParams(dimension_semantics=sem, vmem_limit_bytes=VMEM_LIMIT)


def _resident(shape):
    nd = len(shape)
    return pl.BlockSpec(shape, lambda *_: (0,) * nd, pipeline_mode=pl.Buffered(1))


def _rms(x, g):
    return x * lax.rsqrt(jnp.mean(x * x, axis=-1, keepdims=True) + EPS) * g


def _pv(p, v):
    half = (p.shape[1] // 2) // LANE * LANE
    return (jnp.dot(p[:, :half], v[:half], preferred_element_type=F32)
            + jnp.dot(p[:, half:], v[half:], preferred_element_type=F32))


def _dot_nt(a, b):
    return lax.dot_general(a, b, (((1,), (1,)), ((), ())), preferred_element_type=F32)


def _inproj_kernel(x_ref, g_ref, w_ref, gain_ref, cos_ref, sin_ref, bd_ref, qkv_ref):
    hn = _rms(x_ref[...], g_ref[...]).astype(BF16)
    lane = lax.broadcasted_iota(jnp.int32, (1, CHUNK), 1)
    lower_half = (lane % (2 * ROPE_PAIR)) < ROPE_PAIR
    for c, kind in enumerate(_CHUNK_KINDS):
        cols = slice(c * CHUNK, (c + 1) * CHUNK)
        if c % 2 == 0:
            pair = jnp.dot(hn, w_ref[:, c * CHUNK:(c + 2) * CHUNK], preferred_element_type=F32)
        acc = pair[:, (c % 2) * CHUNK:(c % 2 + 1) * CHUNK]
        if kind != "plain":
            ss = jnp.dot((acc * acc).astype(BF16), bd_ref[...], preferred_element_type=F32)
            acc = acc * lax.rsqrt(ss * (1.0 / HEAD_DIM) + EPS) * gain_ref[:, cols]
        if kind == "rope":
            partner = jnp.where(lower_half, pltpu.roll(acc, CHUNK - ROPE_PAIR, 1),
                                pltpu.roll(acc, ROPE_PAIR, 1))
            acc = acc * cos_ref[...] + partner * sin_ref[...]
        qkv_ref[:, cols] = acc.astype(BF16)


def _inproj(x, g, w, gain, cos, sin, bd, seq, tm=512):
    t, d = x.shape
    n = w.shape[1]
    nseq = seq // tm
    tab = pl.BlockSpec((tm, CHUNK), lambda i: (i % nseq, 0))
    return pl.pallas_call(
        _inproj_kernel, grid=(t // tm,),
        in_specs=[pl.BlockSpec((tm, d), lambda i: (i, 0)), _resident((1, d)), _resident((d, n)),
                  _resident((1, n)), tab, tab, _resident((CHUNK, CHUNK))],
        out_specs=pl.BlockSpec((tm, n), lambda i: (i, 0)),
        out_shape=jax.ShapeDtypeStruct((t, n), BF16), compiler_params=_cparams(("parallel",)),
        name="inproj")(x, g, w, gain, cos, sin, bd)


def _attn_a_kernel(sink_ref, q_ref, k_ref, v_ref, bias_ref, gout_ref, o_ref):
    nb = k_ref.shape[1] // A_BLOCK
    for sub in range(A_QBLOCKS):
        j = pl.program_id(1) * A_QBLOCKS + sub
        rows = slice(sub * A_BLOCK, (sub + 1) * A_BLOCK)
        o_ref[0, rows, :] = _window_block(sink_ref, q_ref[0, rows, :], k_ref, v_ref, bias_ref,
                                          gout_ref, j, nb).astype(o_ref.dtype)


def _window_block(sink_ref, q, k_ref, v_ref, bias_ref, gout_ref, j, nb):
    def three(ref):
        return jnp.concatenate(
            [ref[0, pl.ds(pl.multiple_of(jnp.clip(j + d, 0, nb - 1) * A_BLOCK, A_BLOCK), A_BLOCK), :]
             for d in (-1, 0, 1)], axis=0)
    k = three(k_ref)
    v = three(v_ref)
    col = lax.broadcasted_iota(jnp.int32, (1, 3 * A_BLOCK), 1)
    first_ok = jnp.where(j > 0, 0, A_BLOCK)
    end_ok = jnp.where(j < nb - 1, 3 * A_BLOCK, 2 * A_BLOCK)
    ok = (col >= first_ok) & (col < end_ok)
    g = A_HEADS // A_KV_HEADS
    row = lax.broadcasted_iota(jnp.int32, (g * A_BLOCK, 1), 0)
    outs = []
    for hk in range(A_KV_HEADS):
        kh = k[:, hk * HEAD_DIM:(hk + 1) * HEAD_DIM]
        vh = v[:, hk * HEAD_DIM:(hk + 1) * HEAD_DIM]
        qs = jnp.concatenate([q[:, (hk * g + gi) * HEAD_DIM:(hk * g + gi + 1) * HEAD_DIM]
                              for gi in range(g)], axis=0)
        s = jnp.full((g * A_BLOCK, 1), sink_ref[hk * g], F32)
        for gi in range(1, g):
            s = jnp.where(row >= gi * A_BLOCK, sink_ref[hk * g + gi], s)
        logits = jnp.where(ok, _dot_nt(qs, kh) + bias_ref[hk], NEG)
        m = jnp.maximum(jnp.max(logits, axis=-1, keepdims=True), s)
        p = jnp.exp(logits - m)
        denom = jnp.sum(p, axis=-1, keepdims=True) + jnp.exp(s - m)
        o = _pv(p.astype(BF16), vh) / denom
        outs += [o[gi * A_BLOCK:(gi + 1) * A_BLOCK] for gi in range(g)]
    return _rms(jnp.concatenate(outs, axis=-1), gout_ref[...])


def _attn_a(qkv, sink, bias, gout):
    b, s, _ = qkv.shape
    tq = A_QBLOCKS * A_BLOCK
    return pl.pallas_call(
        _attn_a_kernel, grid=(b, s // tq),
        in_specs=[pl.BlockSpec(memory_space=pltpu.SMEM),
                  pl.BlockSpec((1, tq, A_Q), lambda bi, j: (bi, j, _PERM_OFF["qa"] // A_Q)),
                  pl.BlockSpec((1, s, A_KV), lambda bi, j: (bi, 0, _PERM_OFF["ka"] // A_KV)),
                  pl.BlockSpec((1, s, A_KV), lambda bi, j: (bi, 0, _PERM_OFF["va"] // A_KV)),
                  _resident(bias.shape), _resident((1, A_Q))],
        out_specs=pl.BlockSpec((1, tq, A_Q), lambda bi, j: (bi, j, 0)),
        out_shape=jax.ShapeDtypeStruct((b, s, A_Q), BF16),
        compiler_params=_cparams(("parallel", "arbitrary")), name="attn_a",
    )(sink, qkv, qkv, qkv, bias, gout)


def _attn_b_kernel(ids_ref, q_ref, k_ref, v_ref, bias_ref, gout_ref, o_ref, *, rows, kh):
    del ids_ref
    m = pl.program_id(1)
    span = kh + B_QROWS
    first = jnp.clip(m * B_QROWS - kh // 2, 0, rows - span)
    start = pl.multiple_of(first * GRID_W, GRID_W)
    q = q_ref[0]
    k = k_ref[0, pl.ds(start, span * GRID_W), :]
    v = v_ref[0, pl.ds(start, span * GRID_W), :]
    outs = []
    for h in range(B_HEADS):
        hs = slice(h * HEAD_DIM, (h + 1) * HEAD_DIM)
        logits = _dot_nt(q[:, hs], k[:, hs]) + bias_ref[h, 0]
        m = jnp.max(logits, axis=-1, keepdims=True)
        p = jnp.exp(logits - m)
        denom = jnp.sum(p, axis=-1, keepdims=True)
        outs.append(_pv(p.astype(BF16), v[:, hs]) / denom)
    o = jnp.concatenate(outs, axis=-1)
    o_ref[0] = _rms(o, gout_ref[...]).astype(o_ref.dtype)


def _attn_b(qkv, bias, bias_ids, gout):
    b, s, _ = qkv.shape
    rows = s // GRID_W
    kh = min(NA_MAX_KH, rows)
    nq, nk = B_QROWS * GRID_W, (kh + B_QROWS) * GRID_W
    grid_spec = pltpu.PrefetchScalarGridSpec(
        num_scalar_prefetch=1, grid=(b, rows // B_QROWS),
        in_specs=[pl.BlockSpec((1, nq, B_W), lambda bi, m, ids: (bi, m, _PERM_OFF["qb"] // B_W)),
                  pl.BlockSpec((1, s, B_W), lambda bi, m, ids: (bi, 0, _PERM_OFF["kb"] // B_W)),
                  pl.BlockSpec((1, s, B_W), lambda bi, m, ids: (bi, 0, _PERM_OFF["vb"] // B_W)),
                  pl.BlockSpec((B_HEADS, 1, nq, nk), lambda bi, m, ids: (0, ids[m], 0, 0)),
                  pl.BlockSpec((1, B_W), lambda bi, m, ids: (0, 0))],
        out_specs=pl.BlockSpec((1, nq, B_W), lambda bi, m, ids: (bi, m, 0)))
    return pl.pallas_call(
        functools.partial(_attn_b_kernel, rows=rows, kh=kh), grid_spec=grid_spec,
        out_shape=jax.ShapeDtypeStruct((b, s, B_W), BF16),
        compiler_params=_cparams(("parallel", "arbitrary")), name="attn_b",
    )(bias_ids, qkv, qkv, qkv, bias, gout)


def _attn_c_kernel(q_ref, k_ref, v_ref, gout_ref, o_ref):
    q = q_ref[0]
    k = k_ref[0]
    v = v_ref[0]
    g = C_HEADS // C_KV_HEADS

    def group_logits(hk):
        kh = k[:, hk * HEAD_DIM:(hk + 1) * HEAD_DIM]
        qs = jnp.concatenate([q[:, (hk * g + gi) * HEAD_DIM:(hk * g + gi + 1) * HEAD_DIM]
                              for gi in range(g)], axis=0)
        return _dot_nt(qs, kh)

    outs = []
    ahead = group_logits(0)
    for hk in range(C_KV_HEADS):
        vh = v[:, hk * HEAD_DIM:(hk + 1) * HEAD_DIM]
        logits = ahead
        if hk + 1 < C_KV_HEADS:
            ahead = group_logits(hk + 1)
        m = jnp.max(logits, axis=-1, keepdims=True)
        p = jnp.exp(logits - m)
        denom = jnp.sum(p, axis=-1, keepdims=True)
        o = _pv(p.astype(BF16), vh) / denom
        outs += [o[gi * C_BLOCK:(gi + 1) * C_BLOCK] for gi in range(g)]
    o = jnp.concatenate(outs, axis=-1)
    o_ref[0] = _rms(o, gout_ref[...]).astype(o_ref.dtype)


def _attn_c(qkv, gout):
    b, s, _ = qkv.shape
    return pl.pallas_call(
        _attn_c_kernel, grid=(b, s // C_BLOCK),
        in_specs=[pl.BlockSpec((1, C_BLOCK, C_Q), lambda bi, j: (bi, j, _PERM_OFF["qc"] // C_Q)),
                  pl.BlockSpec((1, s, C_KV), lambda bi, j: (bi, 0, _PERM_OFF["kc"] // C_KV)),
                  pl.BlockSpec((1, s, C_KV), lambda bi, j: (bi, 0, _PERM_OFF["vc"] // C_KV)),
                  _resident((1, C_Q))],
        out_specs=pl.BlockSpec((1, C_BLOCK, C_Q), lambda bi, j: (bi, j, 0)),
        out_shape=jax.ShapeDtypeStruct((b, s, C_Q), BF16),
        compiler_params=_cparams(("parallel", "arbitrary")), name="attn_c",
    )(qkv, qkv, qkv, gout)


def _outproj_kernel(x_ref, a_ref, b_ref, c_ref, w_ref, o_ref):
    acc = jnp.dot(a_ref[...], w_ref[0:A_Q, :], preferred_element_type=F32)
    acc += jnp.dot(b_ref[...], w_ref[A_Q:A_Q + B_W, :], preferred_element_type=F32)
    acc += jnp.dot(c_ref[...], w_ref[A_Q + B_W:D_MIX, :], preferred_element_type=F32)
    o_ref[...] = x_ref[...] + acc


def _outproj(x, ma, mb, mc, w, tm=512):
    t, d = x.shape

    def row(n):
        return pl.BlockSpec((tm, n), lambda i: (i, 0))
    return pl.pallas_call(
        _outproj_kernel, grid=(t // tm,),
        in_specs=[row(d), row(A_Q), row(B_W), row(C_Q), _resident(w.shape)],
        out_specs=row(d), out_shape=jax.ShapeDtypeStruct((t, d), F32),
        compiler_params=_cparams(("parallel",)), name="outproj",
    )(x, ma, mb, mc, w)


def _memkv_kernel(mem_ref, g_ref, w_ref, gk_ref, k_ref, v_ref):
    mn = _rms(mem_ref[0], g_ref[...]).astype(BF16)
    kv = jnp.dot(mn, w_ref[...], preferred_element_type=F32)
    for h in range(MEM_HEADS):
        hs = slice(h * MEM_HEAD_DIM, (h + 1) * MEM_HEAD_DIM)
        k_ref[0, :, hs] = _rms(kv[:, hs], gk_ref[...]).astype(BF16)
    v_ref[0] = kv[:, MEM_W:].astype(BF16)


def _memkv(mem, g, w, gk):
    b, m, d = mem.shape
    out = pl.BlockSpec((1, m, MEM_W), lambda i: (i, 0, 0))
    return pl.pallas_call(
        _memkv_kernel, grid=(b,),
        in_specs=[pl.BlockSpec((1, m, d), lambda i: (i, 0, 0)), _resident((1, d)),
                  _resident(w.shape), _resident((1, MEM_HEAD_DIM))],
        out_specs=[out, out], out_shape=[jax.ShapeDtypeStruct((b, m, MEM_W), BF16)] * 2,
        compiler_params=_cparams(("parallel",)), name="memkv",
    )(mem, g, w, gk)


def _memattn_kernel(x_ref, g_ref, wq_ref, gq_ref, k_ref, v_ref, wo_ref, o_ref):
    x = x_ref[0]
    hn = _rms(x, g_ref[...]).astype(BF16)
    q = jnp.dot(hn, wq_ref[...], preferred_element_type=F32)
    outs = []
    for h in range(MEM_HEADS):
        hs = slice(h * MEM_HEAD_DIM, (h + 1) * MEM_HEAD_DIM)
        qh = _rms(q[:, hs], gq_ref[...]).astype(BF16)
        logits = _dot_nt(qh, k_ref[0, :, hs]) * (MEM_HEAD_DIM ** -0.5)
        m = jnp.max(logits, axis=-1, keepdims=True)
        p = jnp.exp(logits - m)
        denom = jnp.sum(p, axis=-1, keepdims=True)
        outs.append(_pv(p.astype(BF16), v_ref[0, :, hs]) / denom)
    o = jnp.concatenate(outs, axis=-1).astype(BF16)
    o_ref[0] = x + jnp.dot(o, wo_ref[...], preferred_element_type=F32)


def _memattn(x, g, wq, gq, km, vm, wo, tm=512):
    b, s, d = x.shape
    m = km.shape[1]
    row = pl.BlockSpec((1, tm, d), lambda bi, i: (bi, i, 0))
    kv = pl.BlockSpec((1, m, MEM_W), lambda bi, i: (bi, 0, 0))
    return pl.pallas_call(
        _memattn_kernel, grid=(b, s // tm),
        in_specs=[row, _resident((1, d)), _resident(wq.shape), _resident((1, MEM_HEAD_DIM)), kv, kv,
                  _resident(wo.shape)],
        out_specs=row, out_shape=jax.ShapeDtypeStruct((b, s, d), F32),
        compiler_params=_cparams(("parallel", "arbitrary")), name="memattn",
    )(x, g, wq, gq, km, vm, wo)


def _sorting_network(n):
    def merge(lo, hi, r):
        step = r * 2
        if step < hi - lo:
            yield from merge(lo, hi, step)
            yield from merge(lo + r, hi, step)
            yield from ((i, i + r) for i in range(lo + r, hi - r, step))
        else:
            yield (lo, lo + r)

    def sort(lo, hi):
        if hi - lo >= 1:
            mid = lo + (hi - lo) // 2
            yield from sort(lo, mid)
            yield from sort(mid + 1, hi)
            yield from merge(lo, hi, 1)
    return tuple(sort(0, n - 1))


_SORT16 = _sorting_network(PEER_NKEYS // 8)


def _top_values(s):
    depth = PEER_NKEYS // 8
    v = [s[8 * i:8 * (i + 1)] for i in range(depth)]
    for i, j in _SORT16:
        v[i], v[j] = jnp.maximum(v[i], v[j]), jnp.minimum(v[i], v[j])
    tops = []
    for kk in range(PEER_TOPK):
        mk = jnp.max(v[0], axis=0, keepdims=True)
        tops.append(mk)
        pop = v[0] == mk
        for i in range(PEER_TOPK - 1 - kk):
            v[i] = jnp.where(pop, v[i + 1], v[i])
    return tops


def _prefix_count(pred, rows):
    assert (len(rows) + 1) & len(rows) == 0
    outcomes, steps = [], []
    bases = {(): 0}
    step = (len(rows) + 1) // 2
    while step:
        def pick(path):
            if len(path) == len(outcomes):
                return rows[bases[path] + step - 1]
            return jnp.where(outcomes[len(path)], pick(path + (True,)), pick(path + (False,)))
        outcomes.append(pred(pick(())))
        steps.append(step)
        bases = {path + (hit,): base + (step if hit else 0)
                 for path, base in bases.items() for hit in (True, False)}
        step //= 2
    return sum(jnp.where(hit, float(s), 0.0) for hit, s in zip(outcomes, steps))


def _rank_among(tops, s):
    above = _prefix_count(lambda row: s < row, tops[:-1])
    return above + jnp.where(s < tops[-1], 1.0, 0.0)


def _stack8(rows):
    sub = lax.broadcasted_iota(jnp.int32, (8, rows[0].shape[1]), 0)
    out = jnp.broadcast_to(rows[0], sub.shape)
    for kk in range(1, 8):
        out = jnp.where(sub == kk, rows[kk], out)
    return out


def _route_head(s1, s2):
    t1 = _top_values(s1)
    t2 = _top_values(s2)
    r2 = _rank_among(t2, s2)
    t1_lo = _stack8(t1[:8])
    sub = lax.broadcasted_iota(jnp.int32, t1_lo.shape, 0)
    tail = _stack8(t1[8:]) + t2[0]
    lev = [t1_lo + t2[0]]
    for k2 in range(1, PEER_TOPK):
        lev.append(jnp.where(sub < PEER_TOPK // (k2 + 1), t1_lo + t2[k2], -jnp.inf))
    best = []
    for kk in range(PEER_TOPK):
        mk = jnp.max(jnp.maximum(lev[0], tail), axis=0, keepdims=True)
        best.append(mk)
        pop = lev[0] == mk
        tail = jnp.where(tail == mk, -jnp.inf, tail)
        for d in range(PEER_TOPK - 1 - kk):
            lev[d] = jnp.where(pop, lev[d + 1], lev[d])
    tau = best[-1]
    top = t1[0] + t2[0]
    z = best[0] - top
    z = jnp.exp(z)
    for bk in best[1:]:
        z = z + jnp.exp(bk - top)
    def reaches(row):
        return s1 + row >= tau
    n1 = jnp.where(reaches(t2[0]), 1.0, 0.0) + _prefix_count(reaches, t2[1:PEER_TOPK // 2])
    deep = jnp.zeros(tau.shape, F32)
    for k2 in range(PEER_TOPK // 2, PEER_TOPK):
        deep = deep + jnp.where(t1[0] + t2[k2] >= tau, 1.0, 0.0)
    n1 = n1 + jnp.where(s1 == t1[0], deep, 0.0)
    a1 = jnp.exp(s1 - t1[0]) / z
    b2 = jnp.exp(s2 - t2[0])
    return a1, n1, b2, r2


def _router_kernel(x_ref, g_ref, wq_ref, keys_ref, hn_ref, an_ref, b2_ref, r2_ref, q_scr,
                   *, tm):
    hn = _rms(x_ref[...], g_ref[...]).astype(BF16)
    hn_ref[...] = hn
    half = PEER_DKEY // 2
    wide = wq_ref.shape[1] // 2
    for part in range(2):
        q = jnp.dot(hn, wq_ref[:, part * wide:(part + 1) * wide], preferred_element_type=F32)
        for c in range(wide // half):
            q_scr[part * (wide // half) + c] = q[:, c * half:(c + 1) * half].astype(BF16)

    def head(h, carry):
        s1 = _dot_nt(keys_ref[2 * h], q_scr[2 * h])
        s2 = _dot_nt(keys_ref[2 * h + 1], q_scr[2 * h + 1])
        for lc in range(tm // LANE):
            ls = slice(lc * LANE, (lc + 1) * LANE)
            a1, n1, b2, r2 = _route_head(s1[:, ls], s2[:, ls])
            an_ref[0, h, :, ls] = a1
            an_ref[1, h, :, ls] = n1
            fold = (PEER_NKEYS // BF16_ROWS, BF16_ROWS, LANE)
            b2_ref[h, :, :, ls] = b2.astype(BF16).reshape(fold)
            r2_ref[h, :, :, ls] = r2.astype(BF16).reshape(fold)
        return carry
    lax.fori_loop(0, PEER_HEADS, head, 0)


def _router(x, g, wq, keys, tm=512):
    t, d = x.shape
    tiles = PEER_NKEYS // BF16_ROWS
    tab = pl.BlockSpec((2, PEER_HEADS, PEER_NKEYS, tm), lambda i: (0, 0, 0, i))
    tab16 = pl.BlockSpec((PEER_HEADS, tiles, BF16_ROWS, tm), lambda i: (0, 0, 0, i))
    return pl.pallas_call(
        functools.partial(_router_kernel, tm=tm), grid=(t // tm,),
        in_specs=[pl.BlockSpec((tm, d), lambda i: (i, 0)), _resident((1, d)), _resident(wq.shape),
                  _resident(keys.shape)],
        out_specs=[pl.BlockSpec((tm, d), lambda i: (i, 0)), tab, tab16, tab16],
        out_shape=[jax.ShapeDtypeStruct((t, d), BF16)]
        + [jax.ShapeDtypeStruct((2, PEER_HEADS, PEER_NKEYS, t), F32)]
        + [jax.ShapeDtypeStruct((PEER_HEADS, tiles, BF16_ROWS, t), BF16)] * 2,
        scratch_shapes=[pltpu.VMEM((2 * PEER_HEADS, tm, PEER_DKEY // 2), BF16)],
        compiler_params=_cparams(("parallel",)), name="peer_router",
    )(x, g, wq, keys)


def _gelu(x):
    return 0.5 * x * (1.0 + lax.erf(x * (2.0 ** -0.5)))


def _peer_kernel(x_ref, hn_ref, an_ref, b2_ref, r2_ref, u_ref, vt_ref, y_ref, acc_ref, w_ref,
                 *, eb):
    e = pl.program_id(1)
    tb = hn_ref.shape[0]

    @pl.when(e == 0)
    def _():
        acc_ref[...] = jnp.zeros_like(acc_ref)

    tiles = PEER_NKEYS // BF16_ROWS
    slabs = eb // PEER_NKEYS
    per = slabs // PEER_KEY_CHUNKS

    def key_chunk(c):
        return _dot_nt(u_ref[c * per * PEER_NKEYS:(c + 1) * per * PEER_NKEYS, :], hn_ref[...])

    def gate(il):
        g = None
        for h in range(PEER_HEADS):
            n_i = jnp.broadcast_to(an_ref[1, h, il:il + 1, :], (BF16_ROWS, tb)).astype(BF16)[None]
            a_i = jnp.broadcast_to(an_ref[0, h, il:il + 1, :], (BF16_ROWS, tb)).astype(BF16)[None]
            term = jnp.where(r2_ref[h] < n_i, b2_ref[h], 0.0) * a_i
            g = term if g is None else g + term
        return g

    gates = {il: gate(il) for il in range(per)}
    ahead = key_chunk(0)
    for c in range(PEER_KEY_CHUNKS):
        pre = ahead
        if c + 1 < PEER_KEY_CHUNKS:
            ahead = key_chunk(c + 1)
            gates.update({il: gate(il) for il in range((c + 1) * per, (c + 2) * per)})
        for r in range(per):
            il = c * per + r
            act = _gelu(pre[r * PEER_NKEYS:(r + 1) * PEER_NKEYS, :]).astype(BF16)
            w_ref[il * tiles:(il + 1) * tiles] = act.reshape(tiles, BF16_ROWS, tb) * gates.pop(il)
    acc_ref[...] += jnp.dot(vt_ref[0], w_ref[...].reshape(eb, tb), preferred_element_type=F32)

    @pl.when(e == pl.num_programs(1) - 1)
    def _():
        y_ref[...] = x_ref[...] + acc_ref[...].T


def _peer(x, hn, an, b2, r2, u, vt, tb=512):
    t, d = hn.shape
    nblk, _, eb = vt.shape
    tiles = PEER_NKEYS // BF16_ROWS
    key1 = pl.BlockSpec((2, PEER_HEADS, eb // PEER_NKEYS, tb), lambda i, e: (0, 0, e, i))
    key2 = pl.BlockSpec((PEER_HEADS, tiles, BF16_ROWS, tb), lambda i, e: (0, 0, 0, i))
    return pl.pallas_call(
        functools.partial(_peer_kernel, eb=eb), grid=(t // tb, nblk),
        in_specs=[pl.BlockSpec((tb, d), lambda i, e: (i, 0)),
                  pl.BlockSpec((tb, d), lambda i, e: (i, 0)), key1, key2, key2,
                  pl.BlockSpec((eb, d), lambda i, e: (e, 0)),
                  pl.BlockSpec((1, d, eb), lambda i, e: (e, 0, 0))],
        out_specs=pl.BlockSpec((tb, d), lambda i, e: (i, 0)),
        out_shape=jax.ShapeDtypeStruct((t, d), F32),
        scratch_shapes=[pltpu.VMEM((d, tb), F32),
                        pltpu.VMEM((eb // BF16_ROWS, BF16_ROWS, tb), BF16)],
        compiler_params=_cparams(("parallel", "arbitrary")), name="peer_experts",
    )(x, hn, an, b2, r2, u, vt)


def _t5_buckets(rel):
    nb = REL_BUCKETS // 2
    max_exact = nb // 2
    n = np.abs(rel)
    large = max_exact + (np.log(np.maximum(n, 1) / max_exact) / np.log(REL_MAX_DIST / max_exact)
                         * (nb - max_exact)).astype(np.int64)
    large = np.minimum(large, nb - 1)
    return np.where(rel > 0, nb, 0) + np.where(n < max_exact, n, large)


def _lookup(table, idx, n):
    onehot = jnp.asarray(np.arange(n)[:, None] == np.asarray(idx)[None, :], F32)
    return jnp.einsum("...d,dn->...n", table, onehot, precision=lax.Precision.HIGHEST)


def _window_bias(rel_bias):
    span = A_BLOCK + 2 * A_WINDOW
    rel = (np.arange(span)[None, :] - A_WINDOW) - np.arange(A_BLOCK)[:, None]
    table = jnp.concatenate([rel_bias.astype(F32).T, jnp.full((A_HEADS, 1), NEG, F32)], axis=-1)
    idx = np.where(np.abs(rel) <= A_WINDOW, _t5_buckets(rel), REL_BUCKETS)
    bias = _lookup(table, idx.reshape(-1), REL_BUCKETS + 1)
    bias = bias.reshape(A_HEADS, A_BLOCK, span)
    return bias.reshape(A_KV_HEADS, (A_HEADS // A_KV_HEADS) * A_BLOCK, span)


def _neighbourhood_patterns(rows):
    kh = min(NA_MAX_KH, rows)
    span = kh + B_QROWS
    assert rows % B_QROWS == 0 and rows >= span
    m = np.arange(rows // B_QROWS)[:, None, None]
    q_row = m * B_QROWS + np.arange(B_QROWS)[None, :, None]
    k_row = np.clip(m * B_QROWS - kh // 2, 0, rows - span) + np.arange(span)[None, None, :]
    r0 = np.clip(q_row - kh // 2, 0, rows - kh)
    row_ok = (k_row >= r0) & (k_row < r0 + kh)
    dr = np.where(row_ok, k_row - q_row + NA_MAX_KH - 1, 0)
    assert dr.min() >= 0 and dr.max() <= 2 * NA_MAX_KH - 2
    key = np.concatenate([dr.reshape(len(m), -1), row_ok.reshape(len(m), -1)], axis=1)
    _, first_idx, ids = np.unique(key, axis=0, return_index=True, return_inverse=True)
    return ids.reshape(-1).astype(np.int32), dr[first_idx], row_ok[first_idx]


def _neighbourhood_bias(rpb, rows):
    ids, dr, row_ok = _neighbourhood_patterns(rows)
    qc = np.arange(GRID_W)[:, None]
    kc = np.arange(GRID_W)[None, :]
    c_start = np.clip(qc - NA_KW // 2, 0, GRID_W - NA_KW)
    col_ok = (kc >= c_start) & (kc < c_start + NA_KW)
    dc = np.clip(kc - qc + NA_KW - 1, 0, 2 * NA_KW - 2)
    ncol = 2 * NA_KW - 1
    rows_sel = jnp.where(jnp.asarray(row_ok)[None, ..., None], rpb.astype(F32)[:, dr], NEG)
    rows_sel = jnp.concatenate([rows_sel, jnp.full(rows_sel.shape[:-1] + (1,), NEG, F32)], axis=-1)
    tab = _lookup(rows_sel, np.where(col_ok, dc, ncol).reshape(-1), ncol + 1)
    tab = tab.reshape(tab.shape[:-1] + dc.shape)
    tab = jnp.transpose(tab, (0, 1, 2, 4, 3, 5))
    npat, nqr, nkr = dr.shape
    return tab.reshape(B_HEADS, npat, nqr * GRID_W, nkr * GRID_W), jnp.asarray(ids)


def _rope_tables(seq):
    half = HEAD_DIM // 2
    nf = half // 2
    inv = (ROPE_THETA ** (-np.arange(nf) * 2.0 / half)).astype(np.float32).astype(np.float64)
    t = np.arange(seq)
    pos = np.stack([t // GRID_W, t % GRID_W], axis=1).astype(np.float64)
    ang = pos[:, :, None] * inv[None, None, :]
    cos = np.repeat(np.cos(ang), 2, axis=1).reshape(seq, 2, 2, nf)
    sin = np.repeat(np.sin(ang), 2, axis=1).reshape(seq, 2, 2, nf)
    sin = sin * np.array([-1.0, 1.0])[None, None, :, None]
    cos = np.tile(cos.reshape(seq, HEAD_DIM), (1, CHUNK // HEAD_DIM))
    sin = np.tile(sin.reshape(seq, HEAD_DIM), (1, CHUNK // HEAD_DIM))
    return jnp.asarray(cos, F32), jnp.asarray(sin, F32)


def _permute_cols(w):
    return jnp.concatenate([w[..., _REF_SECTIONS[n][0]:_REF_SECTIONS[n][0] + _REF_SECTIONS[n][1]]
                            for n in _PERM_ORDER], axis=-1)


def _proj_gains(ga, gb, gc):
    scale = HEAD_DIM ** -0.5
    one = jnp.ones((HEAD_DIM,), F32)
    per = dict(qa=(ga[0] * scale, A_HEADS), ka=(ga[1], A_KV_HEADS), va=(one, A_KV_HEADS),
               qb=(gb[0] * scale, B_HEADS), kb=(gb[1], B_HEADS), vb=(one, B_HEADS),
               qc=(gc[0] * scale, C_HEADS), kc=(gc[1], C_KV_HEADS), vc=(one, C_KV_HEADS))
    return jnp.concatenate([jnp.tile(per[n][0].astype(F32), per[n][1]) for n in _PERM_ORDER])[None]


def kernel(x, mem, t5_rel_bias, norm_mix, w_in, qk_norm_a, sink_a, qk_norm_b, rpb_b, qk_norm_c,
           out_norm, w_out, norm_mem, norm_mem_kv, w_mem_q, w_mem_kv, qk_norm_mem, w_mem_o,
           norm_ffn, peer_w_q, peer_keys, peer_u, peer_v):
    b, s, d = x.shape
    depth = w_in.shape[0]
    t = b * s
    cos, sin = _rope_tables(s)
    blockdiag = jnp.asarray(np.kron(np.eye(CHUNK // HEAD_DIM), np.ones((HEAD_DIM, HEAD_DIM))), BF16)
    bias_a = _window_bias(t5_rel_bias)
    xf = x.reshape(t, d)
    for l in range(depth):
        qkv = _inproj(xf, norm_mix[l][None], _permute_cols(w_in[l].astype(BF16)),
                      _proj_gains(qk_norm_a[l], qk_norm_b[l], qk_norm_c[l]), cos, sin,
                      blockdiag, s).reshape(b, s, D_IN)
        g_out = out_norm[l]
        ma = _attn_a(qkv, sink_a[l].astype(F32), bias_a, g_out[None, :A_Q])
        bias_b, ids_b = _neighbourhood_bias(rpb_b[l], s // GRID_W)
        mb = _attn_b(qkv, bias_b, ids_b, g_out[None, A_Q:A_Q + B_W])
        mc = _attn_c(qkv, g_out[None, A_Q + B_W:])
        xf = _outproj(xf, ma.reshape(t, A_Q), mb.reshape(t, B_W), mc.reshape(t, C_Q),
                      w_out[l].astype(BF16))
        km, vm = _memkv(mem, norm_mem_kv[l][None], w_mem_kv[l].astype(BF16), qk_norm_mem[l, 1][None])
        xf = _memattn(xf.reshape(b, s, d), norm_mem[l][None], w_mem_q[l].astype(BF16),
                      qk_norm_mem[l, 0][None], km, vm, w_mem_o[l].astype(BF16)).reshape(t, d)
        keys = peer_keys[l].reshape(2 * PEER_HEADS, PEER_NKEYS, PEER_DKEY // 2).astype(BF16)
        hn, an, b2, r2 = _router(xf, norm_ffn[l][None], peer_w_q[l].astype(BF16), keys)
        xf = _peer(xf, hn, an, b2, r2,
                   peer_u[l].astype(BF16),
                   peer_v[l].astype(BF16).reshape(-1, PEER_EB, d).transpose(0, 2, 1))
    return xf.reshape(b, s, d)
```

```python
import functools

import numpy as np
import jax
import jax.numpy as jnp
from jax import lax
from jax.experimental import pallas as pl
from jax.experimental.pallas import tpu as pltpu

F32 = jnp.float32
BF16 = jnp.bfloat16

GRID_W = 64
HEAD_DIM = 64
EPS = 1e-6
NEG = -1e30
A_HEADS, A_KV_HEADS, A_WINDOW, A_BLOCK = 12, 4, 128, 128
B_HEADS, NA_MAX_KH, NA_KW = 8, 8, 16
A_QBLOCKS = 4
B_QROWS = 4
C_HEADS, C_KV_HEADS = 12, 4
C_BLOCK = 256
ROPE_THETA = 10000.0
ROPE_PAIR = HEAD_DIM // 4
REL_BUCKETS, REL_MAX_DIST = 32, 128
MEM_HEADS, MEM_HEAD_DIM = 4, 128
MEM_W = MEM_HEADS * MEM_HEAD_DIM
PEER_HEADS, PEER_NKEYS, PEER_DKEY, PEER_TOPK = 8, 128, 256, 16
A_Q, A_KV = A_HEADS * HEAD_DIM, A_KV_HEADS * HEAD_DIM
B_W = B_HEADS * HEAD_DIM
C_Q, C_KV = C_HEADS * HEAD_DIM, C_KV_HEADS * HEAD_DIM
D_MIX = A_Q + B_W + C_Q
D_IN = A_Q + 2 * A_KV + 3 * B_W + C_Q + 2 * C_KV

LANE = 128
BF16_ROWS = 16
PEER_EB = 1024
PEER_KEY_CHUNKS = 4
CHUNK = 256
VMEM_LIMIT = 56 * 1024 * 1024

_REF_SECTIONS = dict(qa=(0, A_Q), ka=(A_Q, A_KV), va=(A_Q + A_KV, A_KV),
                     qb=(A_Q + 2 * A_KV, B_W), kb=(A_Q + 2 * A_KV + B_W, B_W),
                     vb=(A_Q + 2 * A_KV + 2 * B_W, B_W),
                     qc=(A_Q + 2 * A_KV + 3 * B_W, C_Q),
                     kc=(A_Q + 2 * A_KV + 3 * B_W + C_Q, C_KV),
                     vc=(A_Q + 2 * A_KV + 3 * B_W + C_Q + C_KV, C_KV))
_PERM_ORDER = ("qa", "qc", "qb", "kb", "vb", "ka", "va", "kc", "vc")
_KIND = dict(qa="norm", qc="rope", qb="norm", kb="norm", vb="plain", ka="norm", va="plain",
             kc="rope", vc="plain")
_PERM_OFF = {}
_off = 0
for _n in _PERM_ORDER:
    _PERM_OFF[_n] = _off
    _off += _REF_SECTIONS[_n][1]
_CHUNK_KINDS = []
for _n in _PERM_ORDER:
    _CHUNK_KINDS += [_KIND[_n]] * (_REF_SECTIONS[_n][1] // CHUNK)


def _cparams(sem):
    return pltpu.CompilerParams(dimension_semantics=sem, vmem_limit_bytes=VMEM_LIMIT)


def _resident(shape):
    nd = len(shape)
    return pl.BlockSpec(shape, lambda *_: (0,) * nd, pipeline_mode=pl.Buffered(1))


def _rms(x, g):
    return x * lax.rsqrt(jnp.mean(x * x, axis=-1, keepdims=True) + EPS) * g


def _pv(p, v):
    half = (p.shape[1] // 2) // LANE * LANE
    return (jnp.dot(p[:, :half], v[:half], preferred_element_type=F32)
            + jnp.dot(p[:, half:], v[half:], preferred_element_type=F32))


def _dot_nt(a, b):
    return lax.dot_general(a, b, (((1,), (1,)), ((), ())), preferred_element_type=F32)


def _inproj_kernel(x_ref, g_ref, w_ref, gain_ref, cos_ref, sin_ref, bd_ref, qkv_ref):
    hn = _rms(x_ref[...], g_ref[...]).astype(BF16)
    lane = lax.broadcasted_iota(jnp.int32, (1, CHUNK), 1)
    lower_half = (lane % (2 * ROPE_PAIR)) < ROPE_PAIR
    for c, kind in enumerate(_CHUNK_KINDS):
        cols = slice(c * CHUNK, (c + 1) * CHUNK)
        if c % 2 == 0:
            pair = jnp.dot(hn, w_ref[:, c * CHUNK:(c + 2) * CHUNK], preferred_element_type=F32)
        acc = pair[:, (c % 2) * CHUNK:(c % 2 + 1) * CHUNK]
        if kind != "plain":
            ss = jnp.dot((acc * acc).astype(BF16), bd_ref[...], preferred_element_type=F32)
            acc = acc * lax.rsqrt(ss * (1.0 / HEAD_DIM) + EPS) * gain_ref[:, cols]
        if kind == "rope":
            partner = jnp.where(lower_half, pltpu.roll(acc, CHUNK - ROPE_PAIR, 1),
                                pltpu.roll(acc, ROPE_PAIR, 1))
            acc = acc * cos_ref[...] + partner * sin_ref[...]
        qkv_ref[:, cols] = acc.astype(BF16)


def _inproj(x, g, w, gain, cos, sin, bd, seq, tm=512):
    t, d = x.shape
    n = w.shape[1]
    nseq = seq // tm
    tab = pl.BlockSpec((tm, CHUNK), lambda i: (i % nseq, 0))
    return pl.pallas_call(
        _inproj_kernel, grid=(t // tm,),
        in_specs=[pl.BlockSpec((tm, d), lambda i: (i, 0)), _resident((1, d)), _resident((d, n)),
                  _resident((1, n)), tab, tab, _resident((CHUNK, CHUNK))],
        out_specs=pl.BlockSpec((tm, n), lambda i: (i, 0)),
        out_shape=jax.ShapeDtypeStruct((t, n), BF16), compiler_params=_cparams(("parallel",)),
        name="inproj")(x, g, w, gain, cos, sin, bd)


def _attn_a_kernel(sink_ref, q_ref, k_ref, v_ref, bias_ref, gout_ref, o_ref):
    nb = k_ref.shape[1] // A_BLOCK
    for sub in range(A_QBLOCKS):
        j = pl.program_id(1) * A_QBLOCKS + sub
        rows = slice(sub * A_BLOCK, (sub + 1) * A_BLOCK)
        o_ref[0, rows, :] = _window_block(sink_ref, q_ref[0, rows, :], k_ref, v_ref, bias_ref,
                                          gout_ref, j, nb).astype(o_ref.dtype)


def _window_block(sink_ref, q, k_ref, v_ref, bias_ref, gout_ref, j, nb):
    def three(ref):
        return jnp.concatenate(
            [ref[0, pl.ds(pl.multiple_of(jnp.clip(j + d, 0, nb - 1) * A_BLOCK, A_BLOCK), A_BLOCK), :]
             for d in (-1, 0, 1)], axis=0)
    k = three(k_ref)
    v = three(v_ref)
    col = lax.broadcasted_iota(jnp.int32, (1, 3 * A_BLOCK), 1)
    first_ok = jnp.where(j > 0, 0, A_BLOCK)
    end_ok = jnp.where(j < nb - 1, 3 * A_BLOCK, 2 * A_BLOCK)
    ok = (col >= first_ok) & (col < end_ok)
    g = A_HEADS // A_KV_HEADS
    row = lax.broadcasted_iota(jnp.int32, (g * A_BLOCK, 1), 0)
    outs = []
    for hk in range(A_KV_HEADS):
        kh = k[:, hk * HEAD_DIM:(hk + 1) * HEAD_DIM]
        vh = v[:, hk * HEAD_DIM:(hk + 1) * HEAD_DIM]
        qs = jnp.concatenate([q[:, (hk * g + gi) * HEAD_DIM:(hk * g + gi + 1) * HEAD_DIM]
                              for gi in range(g)], axis=0)
        s = jnp.full((g * A_BLOCK, 1), sink_ref[hk * g], F32)
        for gi in range(1, g):
            s = jnp.where(row >= gi * A_BLOCK, sink_ref[hk * g + gi], s)
        logits = jnp.where(ok, _dot_nt(qs, kh) + bias_ref[hk], NEG)
        m = jnp.maximum(jnp.max(logits, axis=-1, keepdims=True), s)
        p = jnp.exp(logits - m)
        denom = jnp.sum(p, axis=-1, keepdims=True) + jnp.exp(s - m)
        o = _pv(p.astype(BF16), vh) / denom
        outs += [o[gi * A_BLOCK:(gi + 1) * A_BLOCK] for gi in range(g)]
    return _rms(jnp.concatenate(outs, axis=-1), gout_ref[...])


def _attn_a(qkv, sink, bias, gout):
    b, s, _ = qkv.shape
    tq = A_QBLOCKS * A_BLOCK
    return pl.pallas_call(
        _attn_a_kernel, grid=(b, s // tq),
        in_specs=[pl.BlockSpec(memory_space=pltpu.SMEM),
                  pl.BlockSpec((1, tq, A_Q), lambda bi, j: (bi, j, _PERM_OFF["qa"] // A_Q)),
                  pl.BlockSpec((1, s, A_KV), lambda bi, j: (bi, 0, _PERM_OFF["ka"] // A_KV)),
                  pl.BlockSpec((1, s, A_KV), lambda bi, j: (bi, 0, _PERM_OFF["va"] // A_KV)),
                  _resident(bias.shape), _resident((1, A_Q))],
        out_specs=pl.BlockSpec((1, tq, A_Q), lambda bi, j: (bi, j, 0)),
        out_shape=jax.ShapeDtypeStruct((b, s, A_Q), BF16),
        compiler_params=_cparams(("parallel", "arbitrary")), name="attn_a",
    )(sink, qkv, qkv, qkv, bias, gout)


def _attn_b_kernel(ids_ref, q_ref, k_ref, v_ref, bias_ref, gout_ref, o_ref, *, rows, kh):
    del ids_ref
    m = pl.program_id(1)
    span = kh + B_QROWS
    first = jnp.clip(m * B_QROWS - kh // 2, 0, rows - span)
    start = pl.multiple_of(first * GRID_W, GRID_W)
    q = q_ref[0]
    k = k_ref[0, pl.ds(start, span * GRID_W), :]
    v = v_ref[0, pl.ds(start, span * GRID_W), :]
    outs = []
    for h in range(B_HEADS):
        hs = slice(h * HEAD_DIM, (h + 1) * HEAD_DIM)
        logits = _dot_nt(q[:, hs], k[:, hs]) + bias_ref[h, 0]
        m = jnp.max(logits, axis=-1, keepdims=True)
        p = jnp.exp(logits - m)
        denom = jnp.sum(p, axis=-1, keepdims=True)
        outs.append(_pv(p.astype(BF16), v[:, hs]) / denom)
    o = jnp.concatenate(outs, axis=-1)
    o_ref[0] = _rms(o, gout_ref[...]).astype(o_ref.dtype)


def _attn_b(qkv, bias, bias_ids, gout):
    b, s, _ = qkv.shape
    rows = s // GRID_W
    kh = min(NA_MAX_KH, rows)
    nq, nk = B_QROWS * GRID_W, (kh + B_QROWS) * GRID_W
    grid_spec = pltpu.PrefetchScalarGridSpec(
        num_scalar_prefetch=1, grid=(b, rows // B_QROWS),
        in_specs=[pl.BlockSpec((1, nq, B_W), lambda bi, m, ids: (bi, m, _PERM_OFF["qb"] // B_W)),
                  pl.BlockSpec((1, s, B_W), lambda bi, m, ids: (bi, 0, _PERM_OFF["kb"] // B_W)),
                  pl.BlockSpec((1, s, B_W), lambda bi, m, ids: (bi, 0, _PERM_OFF["vb"] // B_W)),
                  pl.BlockSpec((B_HEADS, 1, nq, nk), lambda bi, m, ids: (0, ids[m], 0, 0)),
                  pl.BlockSpec((1, B_W), lambda bi, m, ids: (0, 0))],
        out_specs=pl.BlockSpec((1, nq, B_W), lambda bi, m, ids: (bi, m, 0)))
    return pl.pallas_call(
        functools.partial(_attn_b_kernel, rows=rows, kh=kh), grid_spec=grid_spec,
        out_shape=jax.ShapeDtypeStruct((b, s, B_W), BF16),
        compiler_params=_cparams(("parallel", "arbitrary")), name="attn_b",
    )(bias_ids, qkv, qkv, qkv, bias, gout)


def _attn_c_kernel(q_ref, k_ref, v_ref, gout_ref, o_ref):
    q = q_ref[0]
    k = k_ref[0]
    v = v_ref[0]
    g = C_HEADS // C_KV_HEADS

    def group_logits(hk):
        kh = k[:, hk * HEAD_DIM:(hk + 1) * HEAD_DIM]
        qs = jnp.concatenate([q[:, (hk * g + gi) * HEAD_DIM:(hk * g + gi + 1) * HEAD_DIM]
                              for gi in range(g)], axis=0)
        return _dot_nt(qs, kh)

    outs = []
    ahead = group_logits(0)
    for hk in range(C_KV_HEADS):
        vh = v[:, hk * HEAD_DIM:(hk + 1) * HEAD_DIM]
        logits = ahead
        if hk + 1 < C_KV_HEADS:
            ahead = group_logits(hk + 1)
        m = jnp.max(logits, axis=-1, keepdims=True)
        p = jnp.exp(logits - m)
        denom = jnp.sum(p, axis=-1, keepdims=True)
        o = _pv(p.astype(BF16), vh) / denom
        outs += [o[gi * C_BLOCK:(gi + 1) * C_BLOCK] for gi in range(g)]
    o = jnp.concatenate(outs, axis=-1)
    o_ref[0] = _rms(o, gout_ref[...]).astype(o_ref.dtype)


def _attn_c(qkv, gout):
    b, s, _ = qkv.shape
    return pl.pallas_call(
        _attn_c_kernel, grid=(b, s // C_BLOCK),
        in_specs=[pl.BlockSpec((1, C_BLOCK, C_Q), lambda bi, j: (bi, j, _PERM_OFF["qc"] // C_Q)),
                  pl.BlockSpec((1, s, C_KV), lambda bi, j: (bi, 0, _PERM_OFF["kc"] // C_KV)),
                  pl.BlockSpec((1, s, C_KV), lambda bi, j: (bi, 0, _PERM_OFF["vc"] // C_KV)),
                  _resident((1, C_Q))],
        out_specs=pl.BlockSpec((1, C_BLOCK, C_Q), lambda bi, j: (bi, j, 0)),
        out_shape=jax.ShapeDtypeStruct((b, s, C_Q), BF16),
        compiler_params=_cparams(("parallel", "arbitrary")), name="attn_c",
    )(qkv, qkv, qkv, gout)


def _outproj_kernel(x_ref, a_ref, b_ref, c_ref, w_ref, o_ref):
    acc = jnp.dot(a_ref[...], w_ref[0:A_Q, :], preferred_element_type=F32)
    acc += jnp.dot(b_ref[...], w_ref[A_Q:A_Q + B_W, :], preferred_element_type=F32)
    acc += jnp.dot(c_ref[...], w_ref[A_Q + B_W:D_MIX, :], preferred_element_type=F32)
    o_ref[...] = x_ref[...] + acc


def _outproj(x, ma, mb, mc, w, tm=512):
    t, d = x.shape

    def row(n):
        return pl.BlockSpec((tm, n), lambda i: (i, 0))
    return pl.pallas_call(
        _outproj_kernel, grid=(t // tm,),
        in_specs=[row(d), row(A_Q), row(B_W), row(C_Q), _resident(w.shape)],
        out_specs=row(d), out_shape=jax.ShapeDtypeStruct((t, d), F32),
        compiler_params=_cparams(("parallel",)), name="outproj",
    )(x, ma, mb, mc, w)


def _memkv_kernel(mem_ref, g_ref, w_ref, gk_ref, k_ref, v_ref):
    mn = _rms(mem_ref[0], g_ref[...]).astype(BF16)
    kv = jnp.dot(mn, w_ref[...], preferred_element_type=F32)
    for h in range(MEM_HEADS):
        hs = slice(h * MEM_HEAD_DIM, (h + 1) * MEM_HEAD_DIM)
        k_ref[0, :, hs] = _rms(kv[:, hs], gk_ref[...]).astype(BF16)
    v_ref[0] = kv[:, MEM_W:].astype(BF16)


def _memkv(mem, g, w, gk):
    b, m, d = mem.shape
    out = pl.BlockSpec((1, m, MEM_W), lambda i: (i, 0, 0))
    return pl.pallas_call(
        _memkv_kernel, grid=(b,),
        in_specs=[pl.BlockSpec((1, m, d), lambda i: (i, 0, 0)), _resident((1, d)),
                  _resident(w.shape), _resident((1, MEM_HEAD_DIM))],
        out_specs=[out, out], out_shape=[jax.ShapeDtypeStruct((b, m, MEM_W), BF16)] * 2,
        compiler_params=_cparams(("parallel",)), name="memkv",
    )(mem, g, w, gk)


def _memattn_kernel(x_ref, g_ref, wq_ref, gq_ref, k_ref, v_ref, wo_ref, o_ref):
    x = x_ref[0]
    hn = _rms(x, g_ref[...]).astype(BF16)
    q = jnp.dot(hn, wq_ref[...], preferred_element_type=F32)
    outs = []
    for h in range(MEM_HEADS):
        hs = slice(h * MEM_HEAD_DIM, (h + 1) * MEM_HEAD_DIM)
        qh = _rms(q[:, hs], gq_ref[...]).astype(BF16)
        logits = _dot_nt(qh, k_ref[0, :, hs]) * (MEM_HEAD_DIM ** -0.5)
        m = jnp.max(logits, axis=-1, keepdims=True)
        p = jnp.exp(logits - m)
        denom = jnp.sum(p, axis=-1, keepdims=True)
        outs.append(_pv(p.astype(BF16), v_ref[0, :, hs]) / denom)
    o = jnp.concatenate(outs, axis=-1).astype(BF16)
    o_ref[0] = x + jnp.dot(o, wo_ref[...], preferred_element_type=F32)


def _memattn(x, g, wq, gq, km, vm, wo, tm=512):
    b, s, d = x.shape
    m = km.shape[1]
    row = pl.BlockSpec((1, tm, d), lambda bi, i: (bi, i, 0))
    kv = pl.BlockSpec((1, m, MEM_W), lambda bi, i: (bi, 0, 0))
    return pl.pallas_call(
        _memattn_kernel, grid=(b, s // tm),
        in_specs=[row, _resident((1, d)), _resident(wq.shape), _resident((1, MEM_HEAD_DIM)), kv, kv,
                  _resident(wo.shape)],
        out_specs=row, out_shape=jax.ShapeDtypeStruct((b, s, d), F32),
        compiler_params=_cparams(("parallel", "arbitrary")), name="memattn",
    )(x, g, wq, gq, km, vm, wo)


def _sorting_network(n):
    def merge(lo, hi, r):
        step = r * 2
        if step < hi - lo:
            yield from merge(lo, hi, step)
            yield from merge(lo + r, hi, step)
            yield from ((i, i + r) for i in range(lo + r, hi - r, step))
        else:
            yield (lo, lo + r)

    def sort(lo, hi):
        if hi - lo >= 1:
            mid = lo + (hi - lo) // 2
            yield from sort(lo, mid)
            yield from sort(mid + 1, hi)
            yield from merge(lo, hi, 1)
    return tuple(sort(0, n - 1))


_SORT16 = _sorting_network(PEER_NKEYS // 8)


def _top_values(s):
    depth = PEER_NKEYS // 8
    v = [s[8 * i:8 * (i + 1)] for i in range(depth)]
    for i, j in _SORT16:
        v[i], v[j] = jnp.maximum(v[i], v[j]), jnp.minimum(v[i], v[j])
    tops = []
    for kk in range(PEER_TOPK):
        mk = jnp.max(v[0], axis=0, keepdims=True)
        tops.append(mk)
        pop = v[0] == mk
        for i in range(PEER_TOPK - 1 - kk):
            v[i] = jnp.where(pop, v[i + 1], v[i])
    return tops


def _prefix_count(pred, rows):
    assert (len(rows) + 1) & len(rows) == 0
    outcomes, steps = [], []
    bases = {(): 0}
    step = (len(rows) + 1) // 2
    while step:
        def pick(path):
            if len(path) == len(outcomes):
                return rows[bases[path] + step - 1]
            return jnp.where(outcomes[len(path)], pick(path + (True,)), pick(path + (False,)))
        outcomes.append(pred(pick(())))
        steps.append(step)
        bases = {path + (hit,): base + (step if hit else 0)
                 for path, base in bases.items() for hit in (True, False)}
        step //= 2
    return sum(jnp.where(hit, float(s), 0.0) for hit, s in zip(outcomes, steps))


def _rank_among(tops, s):
    above = _prefix_count(lambda row: s < row, tops[:-1])
    return above + jnp.where(s < tops[-1], 1.0, 0.0)


def _stack8(rows):
    sub = lax.broadcasted_iota(jnp.int32, (8, rows[0].shape[1]), 0)
    out = jnp.broadcast_to(rows[0], sub.shape)
    for kk in range(1, 8):
        out = jnp.where(sub == kk, rows[kk], out)
    return out


def _route_head(s1, s2):
    t1 = _top_values(s1)
    t2 = _top_values(s2)
    r2 = _rank_among(t2, s2)
    t1_lo = _stack8(t1[:8])
    sub = lax.broadcasted_iota(jnp.int32, t1_lo.shape, 0)
    tail = _stack8(t1[8:]) + t2[0]
    lev = [t1_lo + t2[0]]
    for k2 in range(1, PEER_TOPK):
        lev.append(jnp.where(sub < PEER_TOPK // (k2 + 1), t1_lo + t2[k2], -jnp.inf))
    best = []
    for kk in range(PEER_TOPK):
        mk = jnp.max(jnp.maximum(lev[0], tail), axis=0, keepdims=True)
        best.append(mk)
        pop = lev[0] == mk
        tail = jnp.where(tail == mk, -jnp.inf, tail)
        for d in range(PEER_TOPK - 1 - kk):
            lev[d] = jnp.where(pop, lev[d + 1], lev[d])
    tau = best[-1]
    top = t1[0] + t2[0]
    z = best[0] - top
    z = jnp.exp(z)
    for bk in best[1:]:
        z = z + jnp.exp(bk - top)
    def reaches(row):
        return s1 + row >= tau
    n1 = jnp.where(reaches(t2[0]), 1.0, 0.0) + _prefix_count(reaches, t2[1:PEER_TOPK // 2])
    deep = jnp.zeros(tau.shape, F32)
    for k2 in range(PEER_TOPK // 2, PEER_TOPK):
        deep = deep + jnp.where(t1[0] + t2[k2] >= tau, 1.0, 0.0)
    n1 = n1 + jnp.where(s1 == t1[0], deep, 0.0)
    a1 = jnp.exp(s1 - t1[0]) / z
    b2 = jnp.exp(s2 - t2[0])
    return a1, n1, b2, r2


def _router_kernel(x_ref, g_ref, wq_ref, keys_ref, hn_ref, a1_ref, n1_ref, b2_ref, r2_ref, *, tm):
    hn = _rms(x_ref[...], g_ref[...]).astype(BF16)
    hn_ref[...] = hn
    half = PEER_DKEY // 2
    pair = 2 * PEER_DKEY

    def project(p):
        return jnp.dot(hn, wq_ref[:, p * pair:(p + 1) * pair], preferred_element_type=F32)

    ahead = project(0)
    for p in range(PEER_HEADS // 2):
        q = ahead.astype(BF16)
        if p + 1 < PEER_HEADS // 2:
            ahead = project(p + 1)
        for h in (2 * p, 2 * p + 1):
            base = (h - 2 * p) * PEER_DKEY
            s1 = _dot_nt(keys_ref[2 * h], q[:, base:base + half])
            s2 = _dot_nt(keys_ref[2 * h + 1], q[:, base + half:base + 2 * half])
            for lc in range(tm // LANE):
                ls = slice(lc * LANE, (lc + 1) * LANE)
                a1, n1, b2, r2 = _route_head(s1[:, ls], s2[:, ls])
                a1_ref[h, :, ls] = a1
                n1_ref[h, :, ls] = n1
                fold = (PEER_NKEYS // BF16_ROWS, BF16_ROWS, LANE)
                b2_ref[h, :, :, ls] = b2.astype(BF16).reshape(fold)
                r2_ref[h, :, :, ls] = r2.astype(BF16).reshape(fold)


def _router(x, g, wq, keys, tm=512):
    t, d = x.shape
    tiles = PEER_NKEYS // BF16_ROWS
    tab = pl.BlockSpec((PEER_HEADS, PEER_NKEYS, tm), lambda i: (0, 0, i))
    tab16 = pl.BlockSpec((PEER_HEADS, tiles, BF16_ROWS, tm), lambda i: (0, 0, 0, i))
    return pl.pallas_call(
        functools.partial(_router_kernel, tm=tm), grid=(t // tm,),
        in_specs=[pl.BlockSpec((tm, d), lambda i: (i, 0)), _resident((1, d)), _resident(wq.shape),
                  _resident(keys.shape)],
        out_specs=[pl.BlockSpec((tm, d), lambda i: (i, 0)), tab, tab, tab16, tab16],
        out_shape=[jax.ShapeDtypeStruct((t, d), BF16)]
        + [jax.ShapeDtypeStruct((PEER_HEADS, PEER_NKEYS, t), F32)] * 2
        + [jax.ShapeDtypeStruct((PEER_HEADS, tiles, BF16_ROWS, t), BF16)] * 2,
        compiler_params=_cparams(("parallel",)), name="peer_router",
    )(x, g, wq, keys)


def _gelu(x):
    return 0.5 * x * (1.0 + lax.erf(x * (2.0 ** -0.5)))


def _peer_kernel(x_ref, hn_ref, a1_ref, n1_ref, b2_ref, r2_ref, u_ref, vt_ref, y_ref, acc_ref, w_ref,
                 *, eb):
    e = pl.program_id(1)
    tb = hn_ref.shape[0]

    @pl.when(e == 0)
    def _():
        acc_ref[...] = jnp.zeros_like(acc_ref)

    tiles = PEER_NKEYS // BF16_ROWS
    slabs = eb // PEER_NKEYS
    per = slabs // PEER_KEY_CHUNKS

    def key_chunk(c):
        return _dot_nt(u_ref[c * per * PEER_NKEYS:(c + 1) * per * PEER_NKEYS, :], hn_ref[...])

    def gate(il):
        g = None
        for h in range(PEER_HEADS):
            n_i = jnp.broadcast_to(n1_ref[h, il:il + 1, :], (BF16_ROWS, tb)).astype(BF16)[None]
            a_i = jnp.broadcast_to(a1_ref[h, il:il + 1, :], (BF16_ROWS, tb)).astype(BF16)[None]
            term = jnp.where(r2_ref[h] < n_i, b2_ref[h], 0.0) * a_i
            g = term if g is None else g + term
        return g

    gates = {il: gate(il) for il in range(per)}
    ahead = key_chunk(0)
    for c in range(PEER_KEY_CHUNKS):
        pre = ahead
        if c + 1 < PEER_KEY_CHUNKS:
            ahead = key_chunk(c + 1)
            gates.update({il: gate(il) for il in range((c + 1) * per, (c + 2) * per)})
        for r in range(per):
            il = c * per + r
            act = _gelu(pre[r * PEER_NKEYS:(r + 1) * PEER_NKEYS, :]).astype(BF16)
            w_ref[il * tiles:(il + 1) * tiles] = act.reshape(tiles, BF16_ROWS, tb) * gates.pop(il)
    acc_ref[...] += jnp.dot(vt_ref[0], w_ref[...].reshape(eb, tb), preferred_element_type=F32)

    @pl.when(e == pl.num_programs(1) - 1)
    def _():
        y_ref[...] = x_ref[...] + acc_ref[...].T


def _peer(x, hn, a1, n1, b2, r2, u, vt, tb=512):
    t, d = hn.shape
    nblk, _, eb = vt.shape
    tiles = PEER_NKEYS // BF16_ROWS
    key1 = pl.BlockSpec((PEER_HEADS, eb // PEER_NKEYS, tb), lambda i, e: (0, e, i))
    key2 = pl.BlockSpec((PEER_HEADS, tiles, BF16_ROWS, tb), lambda i, e: (0, 0, 0, i))
    return pl.pallas_call(
        functools.partial(_peer_kernel, eb=eb), grid=(t // tb, nblk),
        in_specs=[pl.BlockSpec((tb, d), lambda i, e: (i, 0)),
                  pl.BlockSpec((tb, d), lambda i, e: (i, 0)), key1, key1, key2, key2,
                  pl.BlockSpec((eb, d), lambda i, e: (e, 0)),
                  pl.BlockSpec((1, d, eb), lambda i, e: (e, 0, 0))],
        out_specs=pl.BlockSpec((tb, d), lambda i, e: (i, 0)),
        out_shape=jax.ShapeDtypeStruct((t, d), F32),
        scratch_shapes=[pltpu.VMEM((d, tb), F32),
                        pltpu.VMEM((eb // BF16_ROWS, BF16_ROWS, tb), BF16)],
        compiler_params=_cparams(("parallel", "arbitrary")), name="peer_experts",
    )(x, hn, a1, n1, b2, r2, u, vt)


def _t5_buckets(rel):
    nb = REL_BUCKETS // 2
    max_exact = nb // 2
    n = np.abs(rel)
    large = max_exact + (np.log(np.maximum(n, 1) / max_exact) / np.log(REL_MAX_DIST / max_exact)
                         * (nb - max_exact)).astype(np.int64)
    large = np.minimum(large, nb - 1)
    return np.where(rel > 0, nb, 0) + np.where(n < max_exact, n, large)


def _lookup(table, idx, n):
    onehot = jnp.asarray(np.arange(n)[:, None] == np.asarray(idx)[None, :], F32)
    return jnp.einsum("...d,dn->...n", table, onehot, precision=lax.Precision.HIGHEST)


def _window_bias(rel_bias):
    span = A_BLOCK + 2 * A_WINDOW
    rel = (np.arange(span)[None, :] - A_WINDOW) - np.arange(A_BLOCK)[:, None]
    table = jnp.concatenate([rel_bias.astype(F32).T, jnp.full((A_HEADS, 1), NEG, F32)], axis=-1)
    idx = np.where(np.abs(rel) <= A_WINDOW, _t5_buckets(rel), REL_BUCKETS)
    bias = _lookup(table, idx.reshape(-1), REL_BUCKETS + 1)
    bias = bias.reshape(A_HEADS, A_BLOCK, span)
    return bias.reshape(A_KV_HEADS, (A_HEADS // A_KV_HEADS) * A_BLOCK, span)


def _neighbourhood_patterns(rows):
    kh = min(NA_MAX_KH, rows)
    span = kh + B_QROWS
    assert rows % B_QROWS == 0 and rows >= span
    m = np.arange(rows // B_QROWS)[:, None, None]
    q_row = m * B_QROWS + np.arange(B_QROWS)[None, :, None]
    k_row = np.clip(m * B_QROWS - kh // 2, 0, rows - span) + np.arange(span)[None, None, :]
    r0 = np.clip(q_row - kh // 2, 0, rows - kh)
    row_ok = (k_row >= r0) & (k_row < r0 + kh)
    dr = np.where(row_ok, k_row - q_row + NA_MAX_KH - 1, 0)
    assert dr.min() >= 0 and dr.max() <= 2 * NA_MAX_KH - 2
    key = np.concatenate([dr.reshape(len(m), -1), row_ok.reshape(len(m), -1)], axis=1)
    _, first_idx, ids = np.unique(key, axis=0, return_index=True, return_inverse=True)
    return ids.reshape(-1).astype(np.int32), dr[first_idx], row_ok[first_idx]


def _neighbourhood_bias(rpb, rows):
    ids, dr, row_ok = _neighbourhood_patterns(rows)
    qc = np.arange(GRID_W)[:, None]
    kc = np.arange(GRID_W)[None, :]
    c_start = np.clip(qc - NA_KW // 2, 0, GRID_W - NA_KW)
    col_ok = (kc >= c_start) & (kc < c_start + NA_KW)
    dc = np.clip(kc - qc + NA_KW - 1, 0, 2 * NA_KW - 2)
    ncol = 2 * NA_KW - 1
    rows_sel = jnp.where(jnp.asarray(row_ok)[None, ..., None], rpb.astype(F32)[:, dr], NEG)
    rows_sel = jnp.concatenate([rows_sel, jnp.full(rows_sel.shape[:-1] + (1,), NEG, F32)], axis=-1)
    tab = _lookup(rows_sel, np.where(col_ok, dc, ncol).reshape(-1), ncol + 1)
    tab = tab.reshape(tab.shape[:-1] + dc.shape)
    tab = jnp.transpose(tab, (0, 1, 2, 4, 3, 5))
    npat, nqr, nkr = dr.shape
    return tab.reshape(B_HEADS, npat, nqr * GRID_W, nkr * GRID_W), jnp.asarray(ids)


def _rope_tables(seq):
    half = HEAD_DIM // 2
    nf = half // 2
    inv = (ROPE_THETA ** (-np.arange(nf) * 2.0 / half)).astype(np.float32).astype(np.float64)
    t = np.arange(seq)
    pos = np.stack([t // GRID_W, t % GRID_W], axis=1).astype(np.float64)
    ang = pos[:, :, None] * inv[None, None, :]
    cos = np.repeat(np.cos(ang), 2, axis=1).reshape(seq, 2, 2, nf)
    sin = np.repeat(np.sin(ang), 2, axis=1).reshape(seq, 2, 2, nf)
    sin = sin * np.array([-1.0, 1.0])[None, None, :, None]
    cos = np.tile(cos.reshape(seq, HEAD_DIM), (1, CHUNK // HEAD_DIM))
    sin = np.tile(sin.reshape(seq, HEAD_DIM), (1, CHUNK // HEAD_DIM))
    return jnp.asarray(cos, F32), jnp.asarray(sin, F32)


def _permute_cols(w):
    return jnp.concatenate([w[..., _REF_SECTIONS[n][0]:_REF_SECTIONS[n][0] + _REF_SECTIONS[n][1]]
                            for n in _PERM_ORDER], axis=-1)


def _proj_gains(ga, gb, gc):
    scale = HEAD_DIM ** -0.5
    one = jnp.ones((HEAD_DIM,), F32)
    per = dict(qa=(ga[0] * scale, A_HEADS), ka=(ga[1], A_KV_HEADS), va=(one, A_KV_HEADS),
               qb=(gb[0] * scale, B_HEADS), kb=(gb[1], B_HEADS), vb=(one, B_HEADS),
               qc=(gc[0] * scale, C_HEADS), kc=(gc[1], C_KV_HEADS), vc=(one, C_KV_HEADS))
    return jnp.concatenate([jnp.tile(per[n][0].astype(F32), per[n][1]) for n in _PERM_ORDER])[None]


def kernel(x, mem, t5_rel_bias, norm_mix, w_in, qk_norm_a, sink_a, qk_norm_b, rpb_b, qk_norm_c,
           out_norm, w_out, norm_mem, norm_mem_kv, w_mem_q, w_mem_kv, qk_norm_mem, w_mem_o,
           norm_ffn, peer_w_q, peer_keys, peer_u, peer_v):
    b, s, d = x.shape
    depth = w_in.shape[0]
    t = b * s
    cos, sin = _rope_tables(s)
    blockdiag = jnp.asarray(np.kron(np.eye(CHUNK // HEAD_DIM), np.ones((HEAD_DIM, HEAD_DIM))), BF16)
    bias_a = _window_bias(t5_rel_bias)
    xf = x.reshape(t, d)
    for l in range(depth):
        qkv = _inproj(xf, norm_mix[l][None], _permute_cols(w_in[l].astype(BF16)),
                      _proj_gains(qk_norm_a[l], qk_norm_b[l], qk_norm_c[l]), cos, sin,
                      blockdiag, s).reshape(b, s, D_IN)
        g_out = out_norm[l]
        ma = _attn_a(qkv, sink_a[l].astype(F32), bias_a, g_out[None, :A_Q])
        bias_b, ids_b = _neighbourhood_bias(rpb_b[l], s // GRID_W)
        mb = _attn_b(qkv, bias_b, ids_b, g_out[None, A_Q:A_Q + B_W])
        mc = _attn_c(qkv, g_out[None, A_Q + B_W:])
        xf = _outproj(xf, ma.reshape(t, A_Q), mb.reshape(t, B_W), mc.reshape(t, C_Q),
                      w_out[l].astype(BF16))
        km, vm = _memkv(mem, norm_mem_kv[l][None], w_mem_kv[l].astype(BF16), qk_norm_mem[l, 1][None])
        xf = _memattn(xf.reshape(b, s, d), norm_mem[l][None], w_mem_q[l].astype(BF16),
                      qk_norm_mem[l, 0][None], km, vm, w_mem_o[l].astype(BF16)).reshape(t, d)
        keys = peer_keys[l].reshape(2 * PEER_HEADS, PEER_NKEYS, PEER_DKEY // 2).astype(BF16)
        hn, a1, n1, b2, r2 = _router(xf, norm_ffn[l][None], peer_w_q[l].astype(BF16), keys)
        xf = _peer(xf, hn, a1, n1, b2, r2,
                   peer_u[l].astype(BF16),
                   peer_v[l].astype(BF16).reshape(-1, PEER_EB, d).transpose(0, 2, 1))
    return xf.reshape(b, s, d)
```
